```python
import math
import jax, jax.numpy as jnp
from jax import lax
import numpy as np

D_MODEL = 2048
BATCH = 4
SEQ = 4096
DEPTH = 1

N_META = 16
BLOCK = 128
PAD = BLOCK - N_META
HEAD_DIM = 128
N_FOX_HEADS = 8
N_RET_HEADS = 8
FOX_WIDTH = N_FOX_HEADS * HEAD_DIM
RET_WIDTH = N_RET_HEADS * HEAD_DIM
MIX_WIDTH = FOX_WIDTH + RET_WIDTH
SPLITS = (FOX_WIDTH, FOX_WIDTH, FOX_WIDTH, N_FOX_HEADS, RET_WIDTH, RET_WIDTH, RET_WIDTH, RET_WIDTH)
IN_COLS = sum(SPLITS)
RET_ROPE_BASE = 10000.0
N_GROUPS = 4
EXPERTS_PER_GROUP = 4
N_EXPERTS = N_GROUPS * EXPERTS_PER_GROUP
TOP_K = 2
D_EXPERT = 1024
EPS = 1e-6
MASK_VALUE = -1e30

kernel_name = "hymba_fox_retnet_hmoe"


def rmsnorm(x, g):
    xf = x.astype(jnp.float32)
    y = xf * lax.rsqrt(jnp.mean(xf * xf, axis=-1, keepdims=True) + EPS)
    return (y * g.astype(jnp.float32)).astype(x.dtype)


def to_heads(t, n_heads):
    b, p, _ = t.shape
    return t.reshape(b, p, n_heads, HEAD_DIM).transpose(0, 2, 1, 3)


def rotate_every_two(x):
    x1 = x[..., ::2]
    x2 = x[..., 1::2]
    return jnp.stack((-x2, x1), axis=-1).reshape(x.shape)


def forgetting_attention(q, k, v, log_f, key_valid):
    b, h, p, dh = q.shape
    nblk = p // BLOCK
    cum = jnp.cumsum(log_f, axis=-1)
    scale = HEAD_DIM ** -0.5
    kpos = jnp.arange(p)
    qb = q.reshape(b, h, nblk, BLOCK, dh).transpose(2, 0, 1, 3, 4)
    cb = cum.reshape(b, h, nblk, BLOCK).transpose(2, 0, 1, 3)

    def one_block(args):
        i, q_i, c_i = args
        s = jnp.einsum('bhqd,bhkd->bhqk', q_i, k).astype(jnp.float32) * scale
        s = s + c_i[..., :, None] - cum[:, :, None, :]
        qpos = i * BLOCK + jnp.arange(BLOCK)
        mask = (kpos[None, :] <= qpos[:, None]) & key_valid[None, :]
        s = jnp.where(mask[None, None], s, MASK_VALUE)
        pr = jax.nn.softmax(s, axis=-1)
        return jnp.einsum('bhqk,bhkd->bhqd', pr.astype(v.dtype), v)

    out = lax.map(one_block, (jnp.arange(nblk), qb, cb))
    return out.transpose(1, 2, 0, 3, 4).reshape(b, h, p, dh)


def retention_chunkwise(q, k, v, log_gamma):
    b, h, p, dh = q.shape
    n = p // BLOCK
    dt = q.dtype
    qc = q.reshape(b, h, n, BLOCK, dh)
    kc = k.reshape(b, h, n, BLOCK, dh)
    vc = v.reshape(b, h, n, BLOCK, dh)
    idx = jnp.arange(BLOCK, dtype=jnp.float32)
    lg = log_gamma[:, None]
    diff = idx[:, None] - idx[None, :]
    dmask = jnp.where(diff[None] >= 0, jnp.exp(lg[:, :, None] * jnp.maximum(diff, 0.0)[None]), 0.0)
    scores = jnp.einsum('bhnid,bhnjd->bhnij', qc, kc) * dmask[None, :, None].astype(dt)
    inner = jnp.einsum('bhnij,bhnje->bhnie', scores, vc)
    zeta = jnp.exp(lg * (BLOCK - 1 - idx)).astype(dt)
    kv = jnp.einsum('bhnjd,bhnje->bhnde', kc * zeta[None, :, None, :, None], vc)
    chunk_decay = jnp.exp(log_gamma * BLOCK).astype(dt)

    def step(state, kv_n):
        return state * chunk_decay[None, :, None, None] + kv_n, state

    _, prev = lax.scan(step, jnp.zeros((b, h, dh, dh), dt), kv.transpose(2, 0, 1, 3, 4))
    prev = prev.transpose(1, 2, 0, 3, 4)
    xi = jnp.exp(lg * (idx + 1.0)).astype(dt)
    cross = jnp.einsum('bhnid,bhnde->bhnie', qc * xi[None, :, None, :, None], prev)
    return (inner + cross).reshape(b, h, p, dh)


def hierarchical_moe(hn, w_rg, b_rg, w_re, b_re, w_gate, w_up, w_down):
    b, p, d = hn.shape
    t = hn.reshape(b * p, d)
    gl = (t @ w_rg).astype(jnp.float32) + b_rg.astype(jnp.float32)
    gp = jax.nn.softmax(gl, axis=-1)
    _, g_star = lax.top_k(gl, 1)
    p_group = jnp.take_along_axis(gp, g_star, axis=1)[:, 0]
    el = ((t @ w_re).astype(jnp.float32) + b_re.astype(jnp.float32)).reshape(-1, N_GROUPS, EXPERTS_PER_GROUP)
    el_sel = jnp.take_along_axis(el, g_star[:, :, None], axis=1)[:, 0]
    top_v, top_i = lax.top_k(el_sel, TOP_K)
    w_local = jax.nn.softmax(top_v, axis=-1) * p_group[:, None]
    gidx = g_star * EXPERTS_PER_GROUP + top_i
    combine = jnp.sum(jax.nn.one_hot(gidx, N_EXPERTS, dtype=jnp.float32) * w_local[..., None], axis=1)
    combine = combine.astype(t.dtype)
    y = jnp.zeros_like(t)
    for e in range(N_EXPERTS):
        a = jax.nn.silu(t @ w_gate[e]) * (t @ w_up[e])
        y = y + combine[:, e:e + 1] * (a @ w_down[e])
    return y.reshape(b, p, d)


def setup_inputs(seed: int = 0) -> dict:
    key = jax.random.key(seed)
    ks = jax.random.split(key, 20)
    f32 = jnp.float32
    nrm = lambda k, shape, s: jax.random.normal(k, shape, f32) * s
    gain = lambda k, shape: 1.0 + 0.02 * jax.random.normal(k, shape, f32)
    b_forget = jnp.linspace(1.0, 5.0, N_FOX_HEADS, dtype=f32)[None, :] + nrm(ks[4], (DEPTH, N_FOX_HEADS), 0.1)
    return {
        "x": nrm(ks[0], (BATCH, SEQ, D_MODEL), 1.0),
        "meta_tokens": nrm(ks[1], (N_META, D_MODEL), 1.0),
        "attn_norm_g": gain(ks[2], (DEPTH, D_MODEL)),
        "w_in": nrm(ks[3], (DEPTH, D_MODEL, IN_COLS), D_MODEL ** -0.5),
        "b_forget": b_forget,
        "fox_out_g": gain(ks[5], (DEPTH, FOX_WIDTH)),
        "ret_out_g": gain(ks[6], (DEPTH, RET_WIDTH)),
        "w_out": nrm(ks[7], (DEPTH, MIX_WIDTH, D_MODEL), MIX_WIDTH ** -0.5),
        "ffn_norm_g": gain(ks[8], (DEPTH, D_MODEL)),
        "w_router_group": nrm(ks[9], (DEPTH, D_MODEL, N_GROUPS), D_MODEL ** -0.5),
        "b_router_group": nrm(ks[10], (DEPTH, N_GROUPS), 0.01),
        "w_router_expert": nrm(ks[11], (DEPTH, D_MODEL, N_EXPERTS), D_MODEL ** -0.5),
        "b_router_expert": nrm(ks[12], (DEPTH, N_EXPERTS), 0.01),
        "w_gate": nrm(ks[13], (DEPTH, N_EXPERTS, D_MODEL, D_EXPERT), D_MODEL ** -0.5),
        "w_up": nrm(ks[14], (DEPTH, N_EXPERTS, D_MODEL, D_EXPERT), D_MODEL ** -0.5),
        "w_down": nrm(ks[15], (DEPTH, N_EXPERTS, D_EXPERT, D_MODEL), D_EXPERT ** -0.5),
        "final_norm_g": gain(ks[16], (D_MODEL,)),
    }


def reference(x, meta_tokens, attn_norm_g, w_in, b_forget, fox_out_g, ret_out_g, w_out,
              ffn_norm_g, w_router_group, b_router_group, w_router_expert, b_router_expert,
              w_gate, w_up, w_down, final_norm_g):
    b = x.shape[0]
    dt = x.dtype
    h = jnp.concatenate([
        jnp.zeros((b, PAD, D_MODEL), dt),
        jnp.broadcast_to(meta_tokens.astype(dt)[None], (b, N_META, D_MODEL)),
        x], axis=1)
    p = h.shape[1]
    pos = jnp.arange(p)
    valid = pos >= PAD
    log_gamma = jnp.log(1.0 - 2.0 ** (-5.0 - jnp.arange(N_RET_HEADS, dtype=jnp.float32)))
    angle = 1.0 / (RET_ROPE_BASE ** jnp.linspace(0.0, 1.0, HEAD_DIM // 2, dtype=jnp.float32))
    angle = jnp.repeat(angle, 2)
    phase = (pos - PAD).astype(jnp.float32)[:, None] * angle[None, :]
    sin = jnp.sin(phase).astype(dt)
    cos = jnp.cos(phase).astype(dt)
    offsets = list(np.cumsum(SPLITS)[:-1])

    for l in range(DEPTH):
        hn = rmsnorm(h, attn_norm_g[l])
        u = hn @ w_in[l]
        fq, fk, fv, flog, rq, rk, rv, rg = jnp.split(u, offsets, axis=-1)
        log_f = jax.nn.log_sigmoid(flog.astype(jnp.float32) + b_forget[l].astype(jnp.float32))
        log_f = log_f.transpose(0, 2, 1)
        o_fox = forgetting_attention(to_heads(fq, N_FOX_HEADS), to_heads(fk, N_FOX_HEADS),
                                     to_heads(fv, N_FOX_HEADS), log_f, valid)
        o_fox = rmsnorm(o_fox.transpose(0, 2, 1, 3).reshape(b, p, FOX_WIDTH), fox_out_g[l])
        qr = to_heads(rq, N_RET_HEADS)
        kr = to_heads(rk, N_RET_HEADS) * (HEAD_DIM ** -0.5)
        qr = qr * cos + rotate_every_two(qr) * sin
        kr = (kr * cos + rotate_every_two(kr) * sin) * valid[None, None, :, None].astype(dt)
        o_ret = retention_chunkwise(qr, kr, to_heads(rv, N_RET_HEADS), log_gamma)
        of = o_ret.astype(jnp.float32)
        mu = jnp.mean(of, axis=-1, keepdims=True)
        var = jnp.mean(jnp.square(of - mu), axis=-1, keepdims=True)
        of = (of - mu) * lax.rsqrt(var + EPS)
        of = of.transpose(0, 2, 1, 3).reshape(b, p, RET_WIDTH) * ret_out_g[l].astype(jnp.float32)
        o_ret = (of.astype(dt) * jax.nn.silu(rg))
        h = h + jnp.concatenate([o_fox, o_ret], axis=-1) @ w_out[l]
        hn2 = rmsnorm(h, ffn_norm_g[l])
        h = h + hierarchical_moe(hn2, w_router_group[l], b_router_group[l], w_router_expert[l],
                                 b_router_expert[l], w_gate[l], w_up[l], w_down[l])

    h = rmsnorm(h, final_norm_g)
    return h[:, BLOCK:, :]
```

```python
import functools

import jax
import jax.numpy as jnp
from jax import lax
from jax.experimental import pallas as pl
from jax.experimental.pallas import tpu as pltpu

D_MODEL = 2048
N_META = 16
BLOCK = 128
PAD = BLOCK - N_META
HEAD_DIM = 128
N_FOX_HEADS = 8
N_RET_HEADS = 8
FOX_WIDTH = N_FOX_HEADS * HEAD_DIM
RET_WIDTH = N_RET_HEADS * HEAD_DIM
RET_ROPE_BASE = 10000.0
N_GROUPS = 4
EXPERTS_PER_GROUP = 4
N_EXPERTS = N_GROUPS * EXPERTS_PER_GROUP
D_EXPERT = 1024
EPS = 1e-6
MASK_VALUE = -1e30
SCALE = HEAD_DIM ** -0.5

F32 = jnp.float32
BF16 = jnp.bfloat16
I32 = jnp.int32

LANES = 128
VMEM_LIMIT = 52 * 1024 * 1024

SLAB_FQ, SLAB_FK, SLAB_FV, SLAB_RQ, SLAB_RK, SLAB_RV, SLAB_RG = range(7)
N_SLABS = 7
HEADS_PER_SLAB = 8

INPROJ_TM = 1024
FOX_BQ = 256
ROUTE_TM = 256
ROW_TK = 256
EXPERT_TM = 256


def _params(sem):
    return pltpu.CompilerParams(dimension_semantics=sem, vmem_limit_bytes=VMEM_LIMIT)


def _rms(x, g):
    ms = jnp.mean(x * x, axis=-1, keepdims=True)
    return x * lax.rsqrt(ms + EPS) * g


def _pack_bf16_pair(a, b):
    ua = lax.bitcast_convert_type(a, jnp.uint32)
    ub = lax.bitcast_convert_type(b, jnp.uint32)
    ra = (ua + jnp.uint32(0x7FFF) + ((ua >> 16) & jnp.uint32(1))) >> 16
    rb = (ub + jnp.uint32(0x7FFF) + ((ub >> 16) & jnp.uint32(1))) & jnp.uint32(0xFFFF0000)
    return ra | rb


def _unpack_bf16_pair(w):
    a = lax.bitcast_convert_type(w << 16, F32)
    b = lax.bitcast_convert_type(w & jnp.uint32(0xFFFF0000), F32)
    return a, b


def _inproj_body(x_ref, g_ref, w_ref, wf_ref, o_ref, fl_ref, hn_ref):
    @pl.when(pl.program_id(1) == 0)
    def _():
        hn = _rms(x_ref[...], g_ref[...]).astype(BF16)
        hn_ref[...] = hn
        fl_ref[...] = jnp.dot(hn, wf_ref[...], preferred_element_type=F32)

    acc = jnp.dot(hn_ref[...], w_ref[...], preferred_element_type=F32)
    for hh in range(HEADS_PER_SLAB):
        o_ref[hh] = acc[:, hh * HEAD_DIM:(hh + 1) * HEAD_DIM].astype(BF16)


def _inproj(x2d, g, w_main, w_flog, tm):
    t = x2d.shape[0]
    tn = HEADS_PER_SLAB * HEAD_DIM
    return pl.pallas_call(
        _inproj_body,
        grid=(t // tm, N_SLABS),
        in_specs=[
            pl.BlockSpec((tm, D_MODEL), lambda i, j: (i, 0)),
            pl.BlockSpec((1, D_MODEL), lambda i, j: (0, 0)),
            pl.BlockSpec((D_MODEL, tn), lambda i, j: (0, j)),
            pl.BlockSpec((D_MODEL, LANES), lambda i, j: (0, 0)),
        ],
        out_specs=[
            pl.BlockSpec((HEADS_PER_SLAB, tm, HEAD_DIM), lambda i, j: (j, i, 0)),
            pl.BlockSpec((tm, LANES), lambda i, j: (i, 0)),
        ],
        out_shape=[
            jax.ShapeDtypeStruct((N_SLABS * HEADS_PER_SLAB, t, HEAD_DIM), BF16),
            jax.ShapeDtypeStruct((t, LANES), F32),
        ],
        scratch_shapes=[pltpu.VMEM((tm, D_MODEL), BF16)],
        compiler_params=_params(("parallel", "arbitrary")),
        name="inproj",
    )(x2d, g, w_main, w_flog)


def _gates_body(fl_ref, bf_ref, tri_ref, ccol_ref, crow_ref, carry_ref):
    c = pl.program_id(1)

    @pl.when(c == 0)
    def _():
        carry_ref[...] = jnp.zeros_like(carry_ref)

    z = fl_ref[...] + bf_ref[...]
    lf = jnp.minimum(z, 0.0) - jnp.log(1.0 + jnp.exp(-jnp.abs(z)))
    row = lax.broadcasted_iota(I32, lf.shape, 0)
    lf = jnp.where(jnp.logical_or(c > 0, row >= PAD), lf, 0.0)
    cs = jnp.dot(tri_ref[...], lf, precision=lax.Precision.HIGHEST,
                 preferred_element_type=F32) + carry_ref[...]
    carry_ref[...] = cs[BLOCK - 1:BLOCK, :]
    ccol_ref[...] = cs
    crow_ref[...] = cs.T[0:N_FOX_HEADS, :]


def _gates(fl_all, bf_pad):
    b, p, _ = fl_all.shape
    tri = (lax.broadcasted_iota(I32, (BLOCK, BLOCK), 0)
           >= lax.broadcasted_iota(I32, (BLOCK, BLOCK), 1)).astype(F32)
    return pl.pallas_call(
        _gates_body,
        grid=(b, p // BLOCK),
        in_specs=[
            pl.BlockSpec((None, BLOCK, LANES), lambda bi, c: (bi, c, 0)),
            pl.BlockSpec((1, LANES), lambda bi, c: (0, 0)),
            pl.BlockSpec((BLOCK, BLOCK), lambda bi, c: (0, 0)),
        ],
        out_specs=[
            pl.BlockSpec((None, BLOCK, LANES), lambda bi, c: (bi, c, 0)),
            pl.BlockSpec((None, N_FOX_HEADS, BLOCK), lambda bi, c: (bi, 0, c)),
        ],
        out_shape=[
            jax.ShapeDtypeStruct((b, p, LANES), F32),
            jax.ShapeDtypeStruct((b, N_FOX_HEADS, p), F32),
        ],
        scratch_shapes=[pltpu.VMEM((1, LANES), F32)],
        compiler_params=_params(("parallel", "arbitrary")),
        name="gates",
    )(fl_all, bf_pad, tri)


def _fox_body(q_ref, k_ref, v_ref, km_ref, vm_ref, cca_ref, ccb_ref, crow_ref,
              o_ref, m_ref, l_ref, acc_ref):
    h = pl.program_id(1)
    i = pl.program_id(2)
    bq = FOX_BQ
    q = q_ref[...]

    lane = lax.broadcasted_iota(I32, (BLOCK, LANES), 1)

    def head_col(cc):
        return jnp.sum(jnp.where(lane == h, cc, 0.0), axis=-1, keepdims=True)

    ct = jnp.concatenate([head_col(cca_ref[...]), head_col(ccb_ref[...])], axis=0)

    m_ref[...] = jnp.full(m_ref.shape, MASK_VALUE, F32)
    l_ref[...] = jnp.zeros(l_ref.shape, F32)
    acc_ref[...] = jnp.zeros(acc_ref.shape, F32)

    def step(kb, vb, cs_row, mask):
        s = lax.dot_general(q, kb, (((1,), (1,)), ((), ())), preferred_element_type=F32) * SCALE
        s = s + (ct - cs_row)
        if mask is not None:
            s = jnp.where(mask, s, MASK_VALUE)
        m_prev = m_ref[...]
        m_new = jnp.maximum(m_prev, jnp.max(s, axis=-1, keepdims=True))
        alpha = jnp.exp(m_prev - m_new)
        p = jnp.exp(s - m_new)
        l_ref[...] = alpha * l_ref[...] + jnp.sum(p, axis=-1, keepdims=True)
        acc_ref[...] = alpha * acc_ref[...] + jnp.dot(p.astype(BF16), vb, preferred_element_type=F32)
        m_ref[...] = m_new

    mcol = lax.broadcasted_iota(I32, (bq, BLOCK), 1)
    step(km_ref[...], vm_ref[...], crow_ref[:,pl.ds(0, BLOCK)], mcol >= PAD)

    def full_block(j, carry):
        off = pl.multiple_of(j * bq, bq)
        coff = pl.multiple_of(BLOCK + j * bq, BLOCK)
        step(k_ref[pl.ds(off, bq), :], v_ref[pl.ds(off, bq), :],
             crow_ref[:,pl.ds(coff, bq)], None)
        return carry

    lax.fori_loop(0, i, full_block, 0)

    off = pl.multiple_of(i * bq, bq)
    coff = pl.multiple_of(BLOCK + i * bq, BLOCK)
    r = lax.broadcasted_iota(I32, (bq, bq), 0)
    c = lax.broadcasted_iota(I32, (bq, bq), 1)
    step(k_ref[pl.ds(off, bq), :], v_ref[pl.ds(off, bq), :],
         crow_ref[:,pl.ds(coff, bq)], c <= r)

    o_ref[...] = (acc_ref[...] / l_ref[...]).astype(BF16)


def _fox(u, um, ccol, crow, batch, seq):
    nq = seq // FOX_BQ
    p = seq + BLOCK
    return pl.pallas_call(
        _fox_body,
        grid=(batch, N_FOX_HEADS, nq),
        in_specs=[
            pl.BlockSpec((None, FOX_BQ, HEAD_DIM), lambda b, h, i: (SLAB_FQ * 8 + h, b * nq + i, 0)),
            pl.BlockSpec((None, seq, HEAD_DIM), lambda b, h, i: (SLAB_FK * 8 + h, b, 0)),
            pl.BlockSpec((None, seq, HEAD_DIM), lambda b, h, i: (SLAB_FV * 8 + h, b, 0)),
            pl.BlockSpec((None, BLOCK, HEAD_DIM), lambda b, h, i: (SLAB_FK * 8 + h, 0, 0)),
            pl.BlockSpec((None, BLOCK, HEAD_DIM), lambda b, h, i: (SLAB_FV * 8 + h, 0, 0)),
            pl.BlockSpec((None, BLOCK, LANES), lambda b, h, i: (b, 1 + 2 * i, 0)),
            pl.BlockSpec((None, BLOCK, LANES), lambda b, h, i: (b, 2 + 2 * i, 0)),
            pl.BlockSpec((None, 1, p), lambda b, h, i: (b * N_FOX_HEADS + h, 0, 0)),
        ],
        out_specs=pl.BlockSpec((None, FOX_BQ, HEAD_DIM), lambda b, h, i: (h, b * nq + i, 0)),
        out_shape=jax.ShapeDtypeStruct((N_FOX_HEADS, batch * seq, HEAD_DIM), BF16),
        scratch_shapes=[
            pltpu.VMEM((FOX_BQ, 1), F32),
            pltpu.VMEM((FOX_BQ, 1), F32),
            pltpu.VMEM((FOX_BQ, HEAD_DIM), F32),
        ],
        compiler_params=_params(("parallel", "parallel", "parallel")),
        name="fox",
    )(u, u, u, um, um, ccol, ccol, crow.reshape(batch * N_FOX_HEADS, 1, p))


def _ret_body(decays, rq_ref, rk_ref, rv_ref, rg_ref, mk_ref, mv_ref, cos_ref, sin_ref,
              cosm_ref, sinm_ref, dm_ref, zt_ref, xi_ref, gn_ref, o_ref, s_ref):
    n = pl.program_id(1)
    lane = lax.broadcasted_iota(I32, (BLOCK, HEAD_DIM), 1)
    even = (lane & 1) == 0

    def rope(x, cos, sin):
        rot = jnp.where(even, -pltpu.roll(x, HEAD_DIM - 1, 1), pltpu.roll(x, 1, 1))
        return x * cos + rot * sin

    @pl.when(n == 0)
    def _():
        rowm = lax.broadcasted_iota(I32, (BLOCK, HEAD_DIM), 0)
        for hh in range(N_RET_HEADS):
            km = rope(mk_ref[hh].astype(F32) * SCALE, cosm_ref[...], sinm_ref[...])
            km = jnp.where(rowm >= PAD, km, 0.0)
            kz = (km * zt_ref[hh]).T.astype(BF16)
            s_ref[hh] = jnp.dot(kz, mv_ref[hh], preferred_element_type=F32)

    cos = cos_ref[...]
    sin = sin_ref[...]
    for hh in range(N_RET_HEADS):
        qr = rope(rq_ref[hh].astype(F32), cos, sin)
        kr = rope(rk_ref[hh].astype(F32) * SCALE, cos, sin)
        v = rv_ref[hh]
        sc = lax.dot_general(qr.astype(BF16), kr.astype(BF16), (((1,), (1,)), ((), ())),
                             preferred_element_type=F32) * dm_ref[hh]
        inner = jnp.dot(sc.astype(BF16), v, preferred_element_type=F32)
        s_prev = s_ref[hh]
        cross = jnp.dot((qr * xi_ref[hh]).astype(BF16), s_prev.astype(BF16),
                        preferred_element_type=F32)
        kz = (kr * zt_ref[hh]).T.astype(BF16)
        s_ref[hh] = s_prev * decays[hh] + jnp.dot(kz, v, preferred_element_type=F32)

        o = inner + cross
        mu = jnp.mean(o, axis=-1, keepdims=True)
        d = o - mu
        var = jnp.mean(d * d, axis=-1, keepdims=True)
        on = d * lax.rsqrt(var + EPS) * gn_ref[:, hh * HEAD_DIM:(hh + 1) * HEAD_DIM]
        g = rg_ref[hh].astype(F32)
        silu = g / (1.0 + jnp.exp(-g))
        o_ref[:, hh * HEAD_DIM:(hh + 1) * HEAD_DIM] = (on * silu).astype(BF16)


def _retention(u, um, cos, sin, cosm, sinm, dmask, zeta, xi, decays, gain, batch, seq):
    nc = seq // BLOCK
    hs = (N_RET_HEADS, BLOCK, HEAD_DIM)
    full3 = lambda b, n: (0, 0, 0)
    return pl.pallas_call(
        functools.partial(_ret_body, decays),
        grid=(batch, nc),
        in_specs=[
            pl.BlockSpec(hs, lambda b, n: (SLAB_RQ, b * nc + n, 0)),
            pl.BlockSpec(hs, lambda b, n: (SLAB_RK, b * nc + n, 0)),
            pl.BlockSpec(hs, lambda b, n: (SLAB_RV, b * nc + n, 0)),
            pl.BlockSpec(hs, lambda b, n: (SLAB_RG, b * nc + n, 0)),
            pl.BlockSpec(hs, lambda b, n: (SLAB_RK, 0, 0)),
            pl.BlockSpec(hs, lambda b, n: (SLAB_RV, 0, 0)),
            pl.BlockSpec((BLOCK, HEAD_DIM), lambda b, n: (n, 0)),
            pl.BlockSpec((BLOCK, HEAD_DIM), lambda b, n: (n, 0)),
            pl.BlockSpec((BLOCK, HEAD_DIM), lambda b, n: (0, 0)),
            pl.BlockSpec((BLOCK, HEAD_DIM), lambda b, n: (0, 0)),
            pl.BlockSpec(hs, full3),
            pl.BlockSpec(hs, full3),
            pl.BlockSpec(hs, full3),
            pl.BlockSpec((1, RET_WIDTH), lambda b, n: (0, 0)),
        ],
        out_specs=pl.BlockSpec((BLOCK, RET_WIDTH), lambda b, n: (b * nc + n, 0)),
        out_shape=jax.ShapeDtypeStruct((batch * seq, RET_WIDTH), BF16),
        scratch_shapes=[pltpu.VMEM(hs, F32)],
        compiler_params=_params(("parallel", "arbitrary")),
        name="retention",
    )(u, u, u, u, um, um, cos, sin, cosm, sinm, dmask, zeta, xi, gain)


def _route_body(of_ref, or_ref, x_ref, fg_ref, wo_ref, ng_ref, wr_ref, br_ref, ust_ref,
                h1_ref, hp_ref, ri_ref, rw_ref, cnt_ref, run_ref):
    tm = ROUTE_TM

    @pl.when(pl.program_id(0) == 0)
    def _():
        run_ref[...] = jnp.zeros_like(run_ref)

    of = jnp.concatenate([of_ref[hh].astype(F32) for hh in range(N_FOX_HEADS)], axis=-1)
    ofn = _rms(of, fg_ref[...]).astype(BF16)
    attn = jnp.dot(ofn, wo_ref[0:FOX_WIDTH, :], preferred_element_type=F32)
    attn = attn + jnp.dot(or_ref[...], wo_ref[FOX_WIDTH:, :], preferred_element_type=F32)
    h1 = x_ref[...] + attn
    h1_ref[...] = h1
    hn = _rms(h1, ng_ref[...])
    hp_ref[...] = _pack_bf16_pair(hn[:, :D_MODEL // 2], hn[:, D_MODEL // 2:])

    logits = jnp.dot(hn, wr_ref[...], precision=lax.Precision.HIGHEST,
                     preferred_element_type=F32) + br_ref[...]
    lt = logits.T
    gl = lt[0:N_GROUPS, :]
    el = lt[8:8 + N_EXPERTS, :]

    grow = lax.broadcasted_iota(I32, (N_GROUPS, tm), 0).astype(F32)
    gmax = jnp.max(gl, axis=0, keepdims=True)
    g_star = jnp.min(jnp.where(gl == gmax, grow, float(N_GROUPS)), axis=0, keepdims=True)
    p_group = 1.0 / jnp.sum(jnp.exp(gl - gmax), axis=0, keepdims=True)

    erow_i = lax.broadcasted_iota(I32, (N_EXPERTS, tm), 0)
    erow = erow_i.astype(F32)
    egrp = (erow_i >> 2).astype(F32)
    neg = float("-inf")
    elm = jnp.where(egrp == g_star, el, neg)
    v0 = jnp.max(elm, axis=0, keepdims=True)
    i0 = jnp.min(jnp.where(elm == v0, erow, float(N_EXPERTS)), axis=0, keepdims=True)
    elm2 = jnp.where(erow == i0, neg, elm)
    v1 = jnp.max(elm2, axis=0, keepdims=True)
    i1 = jnp.min(jnp.where(elm2 == v1, erow, float(N_EXPERTS)), axis=0, keepdims=True)
    e10 = jnp.exp(v1 - v0)
    w0 = p_group / (1.0 + e10)
    w1 = p_group * e10 / (1.0 + e10)

    oh0 = erow == i0
    oh1 = erow == i1
    both = jnp.logical_or(oh0, oh1)
    prefix = jnp.dot(both.astype(BF16), ust_ref[...], preferred_element_type=F32)
    base = run_ref[...] + prefix
    rank0 = jnp.sum(jnp.where(oh0, base, 0.0), axis=0, keepdims=True)
    rank1 = jnp.sum(jnp.where(oh1, base, 0.0), axis=0, keepdims=True)
    run_new = run_ref[...] + jnp.sum(both.astype(F32), axis=1, keepdims=True)
    run_ref[...] = run_new
    cnt_ref[...] = run_new

    r8 = lax.broadcasted_iota(I32, (8, tm), 0)
    ri = jnp.where(r8 == 0, i0, jnp.where(r8 == 1, i1, jnp.where(r8 == 2, rank0,
                                                                  jnp.where(r8 == 3, rank1, 0.0))))
    ri_ref[...] = ri.astype(I32)
    r128 = lax.broadcasted_iota(I32, (LANES, tm), 0)
    rw = jnp.where(r128 == 0, w0, jnp.where(r128 == 1, w1, 0.0))
    rw_ref[...] = rw.T


def _route(o_fox, o_ret, x2d, fox_g, w_out, ffn_g, w_router, b_router):
    t = x2d.shape[0]
    tm = ROUTE_TM
    ust = (lax.broadcasted_iota(I32, (tm, tm), 0) < lax.broadcasted_iota(I32, (tm, tm), 1)).astype(BF16)
    const2 = lambda i: (0, 0)
    return pl.pallas_call(
        _route_body,
        grid=(t // tm,),
        in_specs=[
            pl.BlockSpec((N_FOX_HEADS, tm, HEAD_DIM), lambda i: (0, i, 0)),
            pl.BlockSpec((tm, RET_WIDTH), lambda i: (i, 0)),
            pl.BlockSpec((tm, D_MODEL), lambda i: (i, 0)),
            pl.BlockSpec((1, FOX_WIDTH), const2),
            pl.BlockSpec((FOX_WIDTH + RET_WIDTH, D_MODEL), const2),
            pl.BlockSpec((1, D_MODEL), const2),
            pl.BlockSpec((D_MODEL, LANES), const2),
            pl.BlockSpec((1, LANES), const2),
            pl.BlockSpec((tm, tm), const2),
        ],
        out_specs=[
            pl.BlockSpec((tm, D_MODEL), lambda i: (i, 0)),
            pl.BlockSpec((tm, D_MODEL // 2), lambda i: (i, 0)),
            pl.BlockSpec((8, tm), lambda i: (0, i)),
            pl.BlockSpec((tm, LANES), lambda i: (i, 0)),
            pl.BlockSpec((N_EXPERTS, tm), const2),
        ],
        out_shape=[
            jax.ShapeDtypeStruct((t, D_MODEL), F32),
            jax.ShapeDtypeStruct((t, D_MODEL // 2), jnp.uint32),
            jax.ShapeDtypeStruct((8, t), I32),
            jax.ShapeDtypeStruct((t, LANES), F32),
            jax.ShapeDtypeStruct((N_EXPERTS, tm), F32),
        ],
        scratch_shapes=[pltpu.VMEM((N_EXPERTS, tm), F32)],
        compiler_params=_params(("arbitrary",)),
        name="route",
    )(o_fox, o_ret, x2d, fox_g, w_out, ffn_g, w_router, b_router, ust)


def _row_copy(src_hbm, src_row, dst, dst_row, sem):
    return pltpu.make_async_copy(src_hbm.at[pl.ds(src_row, 1)], dst.at[pl.ds(dst_row, 1)], sem)


def _dispatch_body(pos_ref, hp_hbm, xs_init_hbm, xs_hbm, sem):
    del xs_init_hbm
    base = pl.program_id(0) * ROW_TK

    def issue(r, carry):
        _row_copy(hp_hbm, base + r, xs_hbm, pos_ref[0, r], sem).start()
        _row_copy(hp_hbm, base + r, xs_hbm, pos_ref[1, r], sem).start()
        return carry

    lax.fori_loop(0, ROW_TK, issue, 0, unroll=8)

    def drain(r, carry):
        _row_copy(hp_hbm, 0, xs_hbm, 0, sem).wait()
        return carry

    lax.fori_loop(0, 2 * ROW_TK, drain, 0)


def _dispatch(pos3, hp, xs_init):
    nt = pos3.shape[0]
    return pl.pallas_call(
        _dispatch_body,
        grid=(nt,),
        in_specs=[
            pl.BlockSpec((None, 2, ROW_TK), lambda i: (i, 0, 0), memory_space=pltpu.SMEM),
            pl.BlockSpec(memory_space=pl.ANY),
            pl.BlockSpec(memory_space=pl.ANY),
        ],
        out_specs=pl.BlockSpec(memory_space=pl.ANY),
        out_shape=jax.ShapeDtypeStruct(xs_init.shape, xs_init.dtype),
        scratch_shapes=[pltpu.SemaphoreType.DMA(())],
        input_output_aliases={2: 0},
        compiler_params=_params(("arbitrary",)),
        name="dispatch",
    )(pos3, hp, xs_init)


def _expert_body(te_ref, nu_ref, xs_ref, wg_ref, wu_ref, wd_ref, ys_ref):
    del te_ref
    i = pl.program_id(0)

    @pl.when(i < nu_ref[0])
    def _():
        a, b = _unpack_bf16_pair(xs_ref[...])
        x = jnp.concatenate([a.astype(BF16), b.astype(BF16)], axis=-1)
        g = jnp.dot(x, wg_ref[...], preferred_element_type=F32)
        u = jnp.dot(x, wu_ref[...], preferred_element_type=F32)
        act = (g / (1.0 + jnp.exp(-g)) * u).astype(BF16)
        y = jnp.dot(act, wd_ref[...], preferred_element_type=F32)
        ys_ref[...] = _pack_bf16_pair(y[:, :D_MODEL // 2], y[:, D_MODEL // 2:])

    @pl.when(i >= nu_ref[0])
    def _():
        ys_ref[...] = jnp.zeros(ys_ref.shape, ys_ref.dtype)


def _experts(tile_expert, n_used, xs, wg, wu, wd):
    nt = xs.shape[0] // EXPERT_TM
    grid_spec = pltpu.PrefetchScalarGridSpec(
        num_scalar_prefetch=2,
        grid=(nt,),
        in_specs=[
            pl.BlockSpec((EXPERT_TM, D_MODEL // 2), lambda i, te, nu: (i, 0)),
            pl.BlockSpec((None, D_MODEL, D_EXPERT), lambda i, te, nu: (te[i], 0, 0)),
            pl.BlockSpec((None, D_MODEL, D_EXPERT), lambda i, te, nu: (te[i], 0, 0)),
            pl.BlockSpec((None, D_EXPERT, D_MODEL), lambda i, te, nu: (te[i], 0, 0)),
        ],
        out_specs=pl.BlockSpec((EXPERT_TM, D_MODEL // 2), lambda i, te, nu: (i, 0)),
    )
    return pl.pallas_call(
        _expert_body,
        grid_spec=grid_spec,
        out_shape=jax.ShapeDtypeStruct(xs.shape, jnp.uint32),
        compiler_params=_params(("arbitrary",)),
        name="experts",
    )(tile_expert, n_used, xs, wg, wu, wd)


def _combine_body(pos_ref, h1_ref, rw_ref, g_ref, ys_hbm, o_ref, y0_ref, y1_ref, sem):
    def issue(r, carry):
        _row_copy(ys_hbm, pos_ref[0, r], y0_ref, r, sem).start()
        _row_copy(ys_hbm, pos_ref[1, r], y1_ref, r, sem).start()
        return carry

    lax.fori_loop(0, ROW_TK, issue, 0, unroll=8)

    def drain(r, carry):
        _row_copy(ys_hbm, 0, y0_ref, 0, sem).wait()
        return carry

    lax.fori_loop(0, 2 * ROW_TK, drain, 0)

    rw = rw_ref[...]
    w0 = rw[:, 0:1]
    w1 = rw[:, 1:2]
    a0, b0 = _unpack_bf16_pair(y0_ref[...])
    a1, b1 = _unpack_bf16_pair(y1_ref[...])
    moe = jnp.concatenate([w0 * a0 + w1 * a1, w0 * b0 + w1 * b1], axis=-1)
    o_ref[...] = _rms(h1_ref[...] + moe, g_ref[...])


def _combine(pos3, h1, rw, g, ys):
    nt = pos3.shape[0]
    t = h1.shape[0]
    return pl.pallas_call(
        _combine_body,
        grid=(nt,),
        in_specs=[
            pl.BlockSpec((None, 2, ROW_TK), lambda i: (i, 0, 0), memory_space=pltpu.SMEM),
            pl.BlockSpec((ROW_TK, D_MODEL), lambda i: (i, 0)),
            pl.BlockSpec((ROW_TK, LANES), lambda i: (i, 0)),
            pl.BlockSpec((1, D_MODEL), lambda i: (0, 0)),
            pl.BlockSpec(memory_space=pl.ANY),
        ],
        out_specs=pl.BlockSpec((ROW_TK, D_MODEL), lambda i: (i, 0)),
        out_shape=jax.ShapeDtypeStruct((t, D_MODEL), F32),
        scratch_shapes=[
            pltpu.VMEM((ROW_TK, D_MODEL // 2), jnp.uint32),
            pltpu.VMEM((ROW_TK, D_MODEL // 2), jnp.uint32),
            pltpu.SemaphoreType.DMA(()),
        ],
        compiler_params=_params(("arbitrary",)),
        name="combine",
    )(pos3, h1, rw, g, ys)


def _layer(x2d, meta_blk, batch, seq, attn_norm_g, w_in, b_forget, fox_out_g, ret_out_g, w_out,
           ffn_norm_g, w_rg, b_rg, w_re, b_re, w_gate, w_up, w_down, consts):
    t = batch * seq
    fw = FOX_WIDTH
    c0 = 3 * fw
    c1 = c0 + N_FOX_HEADS
    w_main = jnp.concatenate([w_in[:, :c0], w_in[:, c1:]], axis=1).astype(BF16)
    w_flog = jnp.pad(w_in[:, c0:c1], ((0, 0), (0, LANES - N_FOX_HEADS))).astype(BF16)
    g_attn = attn_norm_g.reshape(1, D_MODEL)

    u, flog = _inproj(x2d, g_attn, w_main, w_flog, INPROJ_TM)
    um, flog_m = _inproj(meta_blk, g_attn, w_main, w_flog, BLOCK)

    fl_all = jnp.concatenate(
        [jnp.broadcast_to(flog_m[None], (batch, BLOCK, LANES)), flog.reshape(batch, seq, LANES)], axis=1)
    bf_pad = jnp.pad(b_forget.reshape(1, N_FOX_HEADS), ((0, 0), (0, LANES - N_FOX_HEADS)))
    ccol, crow = _gates(fl_all, bf_pad)

    o_fox = _fox(u, um, ccol, crow, batch, seq)
    o_ret = _retention(u, um, consts["cos"], consts["sin"], consts["cosm"], consts["sinm"],
                       consts["dmask"], consts["zeta"], consts["xi"], consts["decays"],
                       ret_out_g.reshape(1, RET_WIDTH), batch, seq)

    w_router = jnp.zeros((D_MODEL, LANES), F32)
    w_router = w_router.at[:, 0:N_GROUPS].set(w_rg).at[:, 8:8 + N_EXPERTS].set(w_re)
    b_router = jnp.zeros((1, LANES), F32)
    b_router = b_router.at[0, 0:N_GROUPS].set(b_rg).at[0, 8:8 + N_EXPERTS].set(b_re)
    h1, hp, ri, rw, cnt = _route(o_fox, o_ret, x2d, fox_out_g.reshape(1, FOX_WIDTH),
                                 w_out.astype(BF16), ffn_norm_g.reshape(1, D_MODEL), w_router, b_router)

    counts = cnt[:, 0].astype(I32)
    padded = ((counts + EXPERT_TM - 1) // EXPERT_TM) * EXPERT_TM
    ends = jnp.cumsum(padded)
    offs = ends - padded
    n_tiles = 2 * t // EXPERT_TM + N_EXPERTS
    n_used = (ends[-1] // EXPERT_TM).astype(I32)
    starts = jnp.arange(n_tiles, dtype=I32) * EXPERT_TM
    te = jnp.sum((starts[:, None] >= ends[None, :]).astype(I32), axis=1)
    last_e = jnp.max(jnp.where(padded > 0, jnp.arange(N_EXPERTS, dtype=I32), 0))
    te = jnp.minimum(te, last_e).astype(I32)
    pos = jnp.stack([offs[ri[0]] + ri[2], offs[ri[1]] + ri[3]], axis=0)
    pos3 = pos.reshape(2, t // ROW_TK, ROW_TK).transpose(1, 0, 2)

    xs = _dispatch(pos3, hp, jnp.zeros((n_tiles * EXPERT_TM, D_MODEL // 2), jnp.uint32))
    ys = _experts(te, n_used.reshape(1), xs, w_gate.astype(BF16), w_up.astype(BF16),
                  w_down.astype(BF16))
    return pos3, h1, rw, ys


def _constants(seq):
    lg = jnp.log(1.0 - 2.0 ** (-5.0 - jnp.arange(N_RET_HEADS, dtype=F32)))
    angle = 1.0 / (RET_ROPE_BASE ** jnp.linspace(0.0, 1.0, HEAD_DIM // 2, dtype=F32))
    angle = jnp.repeat(angle, 2)
    pos = (jnp.arange(seq + BLOCK) - PAD).astype(F32)
    phase = pos[:, None] * angle[None, :]
    sin = jnp.sin(phase)
    cos = jnp.cos(phase)
    idx = jnp.arange(BLOCK, dtype=F32)
    diff = idx[:, None] - idx[None, :]
    dmask = jnp.where(diff[None] >= 0, jnp.exp(lg[:, None, None] * jnp.maximum(diff, 0.0)[None]), 0.0)
    zeta = jnp.exp(lg[:, None] * (BLOCK - 1 - idx)[None, :])
    xi = jnp.exp(lg[:, None] * (idx + 1.0)[None, :])
    bshape = (N_RET_HEADS, BLOCK, HEAD_DIM)
    return {
        "cos": cos[BLOCK:], "sin": sin[BLOCK:], "cosm": cos[:BLOCK], "sinm": sin[:BLOCK],
        "dmask": dmask,
        "zeta": jnp.broadcast_to(zeta[:, :, None], bshape),
        "xi": jnp.broadcast_to(xi[:, :, None], bshape),
        "decays": tuple(float((1.0 - 2.0 ** (-5.0 - h)) ** BLOCK) for h in range(N_RET_HEADS)),
    }


def kernel(x, meta_tokens, attn_norm_g, w_in, b_forget, fox_out_g, ret_out_g, w_out, ffn_norm_g,
           w_router_group, b_router_group, w_router_expert, b_router_expert, w_gate, w_up, w_down,
           final_norm_g):
    batch, seq, _ = x.shape
    depth = w_in.shape[0]
    assert depth == 1, "single-layer trunk"
    x2d = x.reshape(batch * seq, D_MODEL)
    meta_blk = jnp.concatenate([jnp.zeros((PAD, D_MODEL), x.dtype), meta_tokens.astype(x.dtype)], axis=0)
    consts = _constants(seq)
    pos3, h1, rw, ys = _layer(
        x2d, meta_blk, batch, seq, attn_norm_g[0], w_in[0], b_forget[0], fox_out_g[0], ret_out_g[0],
        w_out[0], ffn_norm_g[0], w_router_group[0], b_router_group[0], w_router_expert[0],
        b_router_expert[0], w_gate[0], w_up[0], w_down[0], consts)
    out = _combine(pos3, h1, rw, final_norm_g.reshape(1, D_MODEL), ys)
    return out.reshape(batch, seq, D_MODEL)
```

```python
import functools

import jax
import jax.numpy as jnp
from jax import lax
from jax.experimental import pallas as pl
from jax.experimental.pallas import tpu as pltpu

D_MODEL = 2048
N_META = 16
BLOCK = 128
PAD = BLOCK - N_META
HEAD_DIM = 128
N_FOX_HEADS = 8
N_RET_HEADS = 8
FOX_WIDTH = N_FOX_HEADS * HEAD_DIM
RET_WIDTH = N_RET_HEADS * HEAD_DIM
RET_ROPE_BASE = 10000.0
N_GROUPS = 4
EXPERTS_PER_GROUP = 4
N_EXPERTS = N_GROUPS * EXPERTS_PER_GROUP
D_EXPERT = 1024
EPS = 1e-6
MASK_VALUE = -1e30
SCALE = HEAD_DIM ** -0.5

F32 = jnp.float32
BF16 = jnp.bfloat16
I32 = jnp.int32

LANES = 128
VMEM_LIMIT = 52 * 1024 * 1024

SLAB_FQ, SLAB_FK, SLAB_FV, SLAB_RQ, SLAB_RK, SLAB_RV, SLAB_RG = range(7)
N_SLABS = 7
HEADS_PER_SLAB = 8

INPROJ_TM = 1024
FOX_BQ = 256
ROUTE_TM = 256
ROW_TK = 256
EXPERT_TM = 256


def _params(sem):
    return pltpu.CompilerParams(dimension_semantics=sem, vmem_limit_bytes=VMEM_LIMIT)


def _rms(x, g):
    ms = jnp.mean(x * x, axis=-1, keepdims=True)
    return x * lax.rsqrt(ms + EPS) * g


def _inproj_body(x_ref, g_ref, w_ref, wf_ref, o_ref, fl_ref, hn_ref):
    @pl.when(pl.program_id(1) == 0)
    def _():
        hn = _rms(x_ref[...], g_ref[...]).astype(BF16)
        hn_ref[...] = hn
        fl_ref[...] = jnp.dot(hn, wf_ref[...], preferred_element_type=F32)

    acc = jnp.dot(hn_ref[...], w_ref[...], preferred_element_type=F32)
    for hh in range(HEADS_PER_SLAB):
        o_ref[hh] = acc[:, hh * HEAD_DIM:(hh + 1) * HEAD_DIM].astype(BF16)


def _inproj(x2d, g, w_main, w_flog, tm):
    t = x2d.shape[0]
    tn = HEADS_PER_SLAB * HEAD_DIM
    return pl.pallas_call(
        _inproj_body,
        grid=(t // tm, N_SLABS),
        in_specs=[
            pl.BlockSpec((tm, D_MODEL), lambda i, j: (i, 0)),
            pl.BlockSpec((1, D_MODEL), lambda i, j: (0, 0)),
            pl.BlockSpec((D_MODEL, tn), lambda i, j: (0, j)),
            pl.BlockSpec((D_MODEL, LANES), lambda i, j: (0, 0)),
        ],
        out_specs=[
            pl.BlockSpec((HEADS_PER_SLAB, tm, HEAD_DIM), lambda i, j: (j, i, 0)),
            pl.BlockSpec((tm, LANES), lambda i, j: (i, 0)),
        ],
        out_shape=[
            jax.ShapeDtypeStruct((N_SLABS * HEADS_PER_SLAB, t, HEAD_DIM), BF16),
            jax.ShapeDtypeStruct((t, LANES), F32),
        ],
        scratch_shapes=[pltpu.VMEM((tm, D_MODEL), BF16)],
        compiler_params=_params(("parallel", "arbitrary")),
        name="inproj",
    )(x2d, g, w_main, w_flog)


def _gates_body(fl_ref, bf_ref, tri_ref, ccol_ref, crow_ref, carry_ref):
    c = pl.program_id(1)

    @pl.when(c == 0)
    def _():
        carry_ref[...] = jnp.zeros_like(carry_ref)

    z = fl_ref[...] + bf_ref[...]
    lf = jnp.minimum(z, 0.0) - jnp.log(1.0 + jnp.exp(-jnp.abs(z)))
    row = lax.broadcasted_iota(I32, lf.shape, 0)
    lf = jnp.where(jnp.logical_or(c > 0, row >= PAD), lf, 0.0)
    cs = jnp.dot(tri_ref[...], lf, precision=lax.Precision.HIGHEST,
                 preferred_element_type=F32) + carry_ref[...]
    carry_ref[...] = cs[BLOCK - 1:BLOCK, :]
    ccol_ref[...] = cs
    crow_ref[...] = cs.T[0:N_FOX_HEADS, :]


def _gates(fl_all, bf_pad):
    b, p, _ = fl_all.shape
    tri = (lax.broadcasted_iota(I32, (BLOCK, BLOCK), 0)
           >= lax.broadcasted_iota(I32, (BLOCK, BLOCK), 1)).astype(F32)
    return pl.pallas_call(
        _gates_body,
        grid=(b, p // BLOCK),
        in_specs=[
            pl.BlockSpec((None, BLOCK, LANES), lambda bi, c: (bi, c, 0)),
            pl.BlockSpec((1, LANES), lambda bi, c: (0, 0)),
            pl.BlockSpec((BLOCK, BLOCK), lambda bi, c: (0, 0)),
        ],
        out_specs=[
            pl.BlockSpec((None, BLOCK, LANES), lambda bi, c: (bi, c, 0)),
            pl.BlockSpec((None, N_FOX_HEADS, BLOCK), lambda bi, c: (bi, 0, c)),
        ],
        out_shape=[
            jax.ShapeDtypeStruct((b, p, LANES), F32),
            jax.ShapeDtypeStruct((b, N_FOX_HEADS, p), F32),
        ],
        scratch_shapes=[pltpu.VMEM((1, LANES), F32)],
        compiler_params=_params(("parallel", "arbitrary")),
        name="gates",
    )(fl_all, bf_pad, tri)


def _fox_body(q_ref, k_ref, v_ref, km_ref, vm_ref, ccol_ref, crow_ref, o_ref,
              vt_ref, vmt_ref, csb_ref, s_ref, m_ref, l_ref, acc_ref):
    h = pl.program_id(1)
    i = pl.program_id(2)
    bq = FOX_BQ
    seq = k_ref.shape[0]

    @pl.when(i == 0)
    def _():
        lane = lax.broadcasted_iota(I32, (BLOCK, LANES), 1)

        def v_chunk(c, carry):
            r0 = pl.multiple_of(c * BLOCK, BLOCK)
            vt_ref[:, pl.ds(r0, BLOCK)] = v_ref[pl.ds(r0, BLOCK), :].astype(F32).T.astype(BF16)
            return carry

        lax.fori_loop(0, seq // BLOCK, v_chunk, 0)
        vmt_ref[...] = vm_ref[...].astype(F32).T.astype(BF16)

        def c_chunk(c, carry):
            r0 = pl.multiple_of(c * BLOCK, BLOCK)
            col = jnp.sum(jnp.where(lane == h, ccol_ref[pl.ds(r0, BLOCK), :], 0.0),
                          axis=-1, keepdims=True)
            csb_ref[pl.ds(r0, BLOCK), :] = jnp.broadcast_to(col, (BLOCK, LANES))
            return carry

        lax.fori_loop(0, (seq + BLOCK) // BLOCK, c_chunk, 0)

    qs = (q_ref[...].astype(F32) * SCALE).astype(BF16)
    ct = crow_ref[:, pl.ds(pl.multiple_of(BLOCK + i * bq, BLOCK), bq)]

    def qk(kb):
        return lax.dot_general(kb, qs, (((1,), (1,)), ((), ())), preferred_element_type=F32)

    def key_bias(row0, n):
        cs = csb_ref[pl.ds(row0, n), :]
        return ct - jnp.concatenate([cs] * (bq // LANES), axis=1)

    sm = qk(km_ref[...]) + key_bias(0, BLOCK)
    sm = jnp.where(lax.broadcasted_iota(I32, (BLOCK, bq), 0) >= PAD, sm, MASK_VALUE)
    m0 = jnp.max(sm, axis=0, keepdims=True)
    pm = jnp.exp(sm - m0)
    m_ref[...] = m0
    l_ref[...] = jnp.sum(pm, axis=0, keepdims=True)
    acc_ref[...] = jnp.dot(vmt_ref[...], pm.astype(BF16), preferred_element_type=F32)
    s_ref[0] = qk(k_ref[0:bq, :])

    def score_next(u, slot):
        off = pl.multiple_of(u * bq, bq)
        s_ref[slot] = qk(k_ref[pl.ds(off, bq), :])

    def process(u, slot, masked):
        off = pl.multiple_of(u * bq, bq)
        s = s_ref[slot] + key_bias(pl.multiple_of(BLOCK + off, BLOCK), bq)
        if masked:
            keep = (lax.broadcasted_iota(I32, (bq, bq), 0) <= lax.broadcasted_iota(I32, (bq, bq), 1))
            s = jnp.where(keep, s, MASK_VALUE)
        m_prev = m_ref[...]
        m_new = jnp.maximum(m_prev, jnp.max(s, axis=0, keepdims=True))
        alpha = jnp.exp(m_prev - m_new)
        p = jnp.exp(s - m_new)
        l_ref[...] = alpha * l_ref[...] + jnp.sum(p, axis=0, keepdims=True)
        acc_ref[...] = alpha * acc_ref[...] + jnp.dot(vt_ref[:, pl.ds(off, bq)], p.astype(BF16),
                                                      preferred_element_type=F32)
        m_ref[...] = m_new

    def pair(jj, carry):
        u0 = 2 * jj
        score_next(u0 + 1, 1)
        process(u0, 0, False)
        score_next(u0 + 2, 0)
        process(u0 + 1, 1, False)
        return carry

    lax.fori_loop(0, i >> 1, pair, 0)

    @pl.when((i & 1) == 1)
    def _():
        score_next(i, 1)
        process(i - 1, 0, False)
        process(i, 1, True)

    @pl.when((i & 1) == 0)
    def _():
        process(i, 0, True)

    o_ref[...] = (acc_ref[...] / l_ref[...]).T.astype(BF16)


def _fox(u, um, ccol, crow, batch, seq):
    nq = seq // FOX_BQ
    p = seq + BLOCK
    return pl.pallas_call(
        _fox_body,
        grid=(batch, N_FOX_HEADS, nq),
        in_specs=[
            pl.BlockSpec((None, FOX_BQ, HEAD_DIM), lambda b, h, i: (SLAB_FQ * 8 + h, b * nq + i, 0)),
            pl.BlockSpec((None, seq, HEAD_DIM), lambda b, h, i: (SLAB_FK * 8 + h, b, 0)),
            pl.BlockSpec((None, seq, HEAD_DIM), lambda b, h, i: (SLAB_FV * 8 + h, b, 0)),
            pl.BlockSpec((None, BLOCK, HEAD_DIM), lambda b, h, i: (SLAB_FK * 8 + h, 0, 0)),
            pl.BlockSpec((None, BLOCK, HEAD_DIM), lambda b, h, i: (SLAB_FV * 8 + h, 0, 0)),
            pl.BlockSpec((None, p, LANES), lambda b, h, i: (b, 0, 0)),
            pl.BlockSpec((None, 1, p), lambda b, h, i: (b * N_FOX_HEADS + h, 0, 0)),
        ],
        out_specs=pl.BlockSpec((None, FOX_BQ, HEAD_DIM), lambda b, h, i: (h, b * nq + i, 0)),
        out_shape=jax.ShapeDtypeStruct((N_FOX_HEADS, batch * seq, HEAD_DIM), BF16),
        scratch_shapes=[
            pltpu.VMEM((HEAD_DIM, seq), BF16),
            pltpu.VMEM((HEAD_DIM, BLOCK), BF16),
            pltpu.VMEM((p, LANES), F32),
            pltpu.VMEM((2, FOX_BQ, FOX_BQ), F32),
            pltpu.VMEM((1, FOX_BQ), F32),
            pltpu.VMEM((1, FOX_BQ), F32),
            pltpu.VMEM((HEAD_DIM, FOX_BQ), F32),
        ],
        compiler_params=_params(("parallel", "parallel", "arbitrary")),
        name="fox",
    )(u, u, u, um, um, ccol, crow.reshape(batch * N_FOX_HEADS, 1, p))


def _ret_body(decays, rq_ref, rk_ref, rv_ref, rg_ref, mk_ref, mv_ref, cos_ref, sin_ref,
              cosm_ref, sinm_ref, dm_ref, zt_ref, xi_ref, gn_ref, o_ref, s_ref):
    n = pl.program_id(1)
    lane = lax.broadcasted_iota(I32, (BLOCK, HEAD_DIM), 1)
    even = (lane & 1) == 0

    def rope(x, cos, sin):
        rot = jnp.where(even, -pltpu.roll(x, HEAD_DIM - 1, 1), pltpu.roll(x, 1, 1))
        return x * cos + rot * sin

    @pl.when(n == 0)
    def _():
        rowm = lax.broadcasted_iota(I32, (BLOCK, HEAD_DIM), 0)
        for hh in range(N_RET_HEADS):
            km = rope(mk_ref[hh].astype(F32) * SCALE, cosm_ref[...], sinm_ref[...])
            km = jnp.where(rowm >= PAD, km, 0.0)
            kz = (km * zt_ref[hh]).T.astype(BF16)
            s_ref[hh] = jnp.dot(kz, mv_ref[hh], preferred_element_type=F32)

    cos = cos_ref[...]
    sin = sin_ref[...]
    for hh in range(N_RET_HEADS):
        qr = rope(rq_ref[hh].astype(F32), cos, sin)
        kr = rope(rk_ref[hh].astype(F32) * SCALE, cos, sin)
        v = rv_ref[hh]
        sc = lax.dot_general(qr.astype(BF16), kr.astype(BF16), (((1,), (1,)), ((), ())),
                             preferred_element_type=F32) * dm_ref[hh]
        inner = jnp.dot(sc.astype(BF16), v, preferred_element_type=F32)
        s_prev = s_ref[hh]
        cross = jnp.dot((qr * xi_ref[hh]).astype(BF16), s_prev.astype(BF16),
                        preferred_element_type=F32)
        kz = (kr * zt_ref[hh]).T.astype(BF16)
        s_ref[hh] = s_prev * decays[hh] + jnp.dot(kz, v, preferred_element_type=F32)

        o = inner + cross
        mu = jnp.mean(o, axis=-1, keepdims=True)
        d = o - mu
        var = jnp.mean(d * d, axis=-1, keepdims=True)
        on = d * lax.rsqrt(var + EPS) * gn_ref[:, hh * HEAD_DIM:(hh + 1) * HEAD_DIM]
        g = rg_ref[hh].astype(F32)
        silu = g / (1.0 + jnp.exp(-g))
        o_ref[:, hh * HEAD_DIM:(hh + 1) * HEAD_DIM] = (on * silu).astype(BF16)


def _retention(u, um, cos, sin, cosm, sinm, dmask, zeta, xi, decays, gain, batch, seq):
    nc = seq // BLOCK
    hs = (N_RET_HEADS, BLOCK, HEAD_DIM)
    full3 = lambda b, n: (0, 0, 0)
    return pl.pallas_call(
        functools.partial(_ret_body, decays),
        grid=(batch, nc),
        in_specs=[
            pl.BlockSpec(hs, lambda b, n: (SLAB_RQ, b * nc + n, 0)),
            pl.BlockSpec(hs, lambda b, n: (SLAB_RK, b * nc + n, 0)),
            pl.BlockSpec(hs, lambda b, n: (SLAB_RV, b * nc + n, 0)),
            pl.BlockSpec(hs, lambda b, n: (SLAB_RG, b * nc + n, 0)),
            pl.BlockSpec(hs, lambda b, n: (SLAB_RK, 0, 0)),
            pl.BlockSpec(hs, lambda b, n: (SLAB_RV, 0, 0)),
            pl.BlockSpec((BLOCK, HEAD_DIM), lambda b, n: (n, 0)),
            pl.BlockSpec((BLOCK, HEAD_DIM), lambda b, n: (n, 0)),
            pl.BlockSpec((BLOCK, HEAD_DIM), lambda b, n: (0, 0)),
            pl.BlockSpec((BLOCK, HEAD_DIM), lambda b, n: (0, 0)),
            pl.BlockSpec(hs, full3),
            pl.BlockSpec(hs, full3),
            pl.BlockSpec(hs, full3),
            pl.BlockSpec((1, RET_WIDTH), lambda b, n: (0, 0)),
        ],
        out_specs=pl.BlockSpec((BLOCK, RET_WIDTH), lambda b, n: (b * nc + n, 0)),
        out_shape=jax.ShapeDtypeStruct((batch * seq, RET_WIDTH), BF16),
        scratch_shapes=[pltpu.VMEM(hs, F32)],
        compiler_params=_params(("parallel", "arbitrary")),
        name="retention",
    )(u, u, u, u, um, um, cos, sin, cosm, sinm, dmask, zeta, xi, gain)


def _route_body(of_ref, or_ref, x_ref, fg_ref, wo_ref, ng_ref, wr_ref, br_ref, ust_ref,
                h1_ref, hn_ref, ri_ref, rw_ref, cnt_ref, run_ref):
    tm = ROUTE_TM

    @pl.when(pl.program_id(0) == 0)
    def _():
        run_ref[...] = jnp.zeros_like(run_ref)

    of = jnp.concatenate([of_ref[hh].astype(F32) for hh in range(N_FOX_HEADS)], axis=-1)
    ofn = _rms(of, fg_ref[...]).astype(BF16)
    attn = jnp.dot(ofn, wo_ref[0:FOX_WIDTH, :], preferred_element_type=F32)
    attn = attn + jnp.dot(or_ref[...], wo_ref[FOX_WIDTH:, :], preferred_element_type=F32)
    h1 = x_ref[...] + attn
    h1_ref[...] = h1
    hn = _rms(h1, ng_ref[...])
    hn_ref[...] = hn

    logits = jnp.dot(hn, wr_ref[...], precision=lax.Precision.HIGHEST,
                     preferred_element_type=F32) + br_ref[...]
    lt = logits.T
    gl = lt[0:N_GROUPS, :]
    el = lt[8:8 + N_EXPERTS, :]

    grow = lax.broadcasted_iota(I32, (N_GROUPS, tm), 0).astype(F32)
    gmax = jnp.max(gl, axis=0, keepdims=True)
    g_star = jnp.min(jnp.where(gl == gmax, grow, float(N_GROUPS)), axis=0, keepdims=True)
    p_group = 1.0 / jnp.sum(jnp.exp(gl - gmax), axis=0, keepdims=True)

    erow_i = lax.broadcasted_iota(I32, (N_EXPERTS, tm), 0)
    erow = erow_i.astype(F32)
    egrp = (erow_i >> 2).astype(F32)
    neg = float("-inf")
    elm = jnp.where(egrp == g_star, el, neg)
    v0 = jnp.max(elm, axis=0, keepdims=True)
    i0 = jnp.min(jnp.where(elm == v0, erow, float(N_EXPERTS)), axis=0, keepdims=True)
    elm2 = jnp.where(erow == i0, neg, elm)
    v1 = jnp.max(elm2, axis=0, keepdims=True)
    i1 = jnp.min(jnp.where(elm2 == v1, erow, float(N_EXPERTS)), axis=0, keepdims=True)
    e10 = jnp.exp(v1 - v0)
    w0 = p_group / (1.0 + e10)
    w1 = p_group * e10 / (1.0 + e10)

    oh0 = erow == i0
    oh1 = erow == i1
    both = jnp.logical_or(oh0, oh1)
    prefix = jnp.dot(both.astype(BF16), ust_ref[...], preferred_element_type=F32)
    base = run_ref[...] + prefix
    rank0 = jnp.sum(jnp.where(oh0, base, 0.0), axis=0, keepdims=True)
    rank1 = jnp.sum(jnp.where(oh1, base, 0.0), axis=0, keepdims=True)
    run_new = run_ref[...] + jnp.sum(both.astype(F32), axis=1, keepdims=True)
    run_ref[...] = run_new
    cnt_ref[...] = run_new

    r8 = lax.broadcasted_iota(I32, (8, tm), 0)
    ri = jnp.where(r8 == 0, i0, jnp.where(r8 == 1, i1, jnp.where(r8 == 2, rank0,
                                                                  jnp.where(r8 == 3, rank1, 0.0))))
    ri_ref[...] = ri.astype(I32)
    r128 = lax.broadcasted_iota(I32, (LANES, tm), 0)
    rw = jnp.where(r128 == 0, w0, jnp.where(r128 == 1, w1, 0.0))
    rw_ref[...] = rw.T


def _route(o_fox, o_ret, x2d, fox_g, w_out, ffn_g, w_router, b_router):
    t = x2d.shape[0]
    tm = ROUTE_TM
    ust = (lax.broadcasted_iota(I32, (tm, tm), 0) < lax.broadcasted_iota(I32, (tm, tm), 1)).astype(BF16)
    const2 = lambda i: (0, 0)
    return pl.pallas_call(
        _route_body,
        grid=(t // tm,),
        in_specs=[
            pl.BlockSpec((N_FOX_HEADS, tm, HEAD_DIM), lambda i: (0, i, 0)),
            pl.BlockSpec((tm, RET_WIDTH), lambda i: (i, 0)),
            pl.BlockSpec((tm, D_MODEL), lambda i: (i, 0)),
            pl.BlockSpec((1, FOX_WIDTH), const2),
            pl.BlockSpec((FOX_WIDTH + RET_WIDTH, D_MODEL), const2),
            pl.BlockSpec((1, D_MODEL), const2),
            pl.BlockSpec((D_MODEL, LANES), const2),
            pl.BlockSpec((1, LANES), const2),
            pl.BlockSpec((tm, tm), const2),
        ],
        out_specs=[
            pl.BlockSpec((tm, D_MODEL), lambda i: (i, 0)),
            pl.BlockSpec((tm, D_MODEL), lambda i: (i, 0)),
            pl.BlockSpec((8, tm), lambda i: (0, i)),
            pl.BlockSpec((tm, LANES), lambda i: (i, 0)),
            pl.BlockSpec((N_EXPERTS, tm), const2),
        ],
        out_shape=[
            jax.ShapeDtypeStruct((t, D_MODEL), F32),
            jax.ShapeDtypeStruct((t, D_MODEL), F32),
            jax.ShapeDtypeStruct((8, t), I32),
            jax.ShapeDtypeStruct((t, LANES), F32),
            jax.ShapeDtypeStruct((N_EXPERTS, tm), F32),
        ],
        scratch_shapes=[pltpu.VMEM((N_EXPERTS, tm), F32)],
        compiler_params=_params(("arbitrary",)),
        name="route",
    )(o_fox, o_ret, x2d, fox_g, w_out, ffn_g, w_router, b_router, ust)


def _row_copy(src, src_row, dst, dst_row, sem):
    return pltpu.make_async_copy(src.at[pl.ds(src_row, 1)], dst.at[pl.ds(dst_row, 1)], sem)


def _dispatch_body(pos_ref, hn_ref, xs_init_hbm, xs_hbm, sem):
    del xs_init_hbm

    def issue(r, carry):
        _row_copy(hn_ref, r, xs_hbm, pos_ref[0, r], sem).start()
        _row_copy(hn_ref, r, xs_hbm, pos_ref[1, r], sem).start()
        return carry

    lax.fori_loop(0, ROW_TK, issue, 0, unroll=8)
    for _ in range(2):
        pltpu.make_async_copy(hn_ref, xs_hbm.at[pl.ds(0, ROW_TK)], sem).wait()


def _dispatch(pos3, hn, xs_init):
    nt = pos3.shape[0]
    return pl.pallas_call(
        _dispatch_body,
        grid=(nt,),
        in_specs=[
            pl.BlockSpec((None, 2, ROW_TK), lambda i: (i, 0, 0), memory_space=pltpu.SMEM),
            pl.BlockSpec((ROW_TK, D_MODEL), lambda i: (i, 0)),
            pl.BlockSpec(memory_space=pl.ANY),
        ],
        out_specs=pl.BlockSpec(memory_space=pl.ANY),
        out_shape=jax.ShapeDtypeStruct(xs_init.shape, xs_init.dtype),
        scratch_shapes=[pltpu.SemaphoreType.DMA(())],
        input_output_aliases={2: 0},
        compiler_params=_params(("arbitrary",)),
        name="dispatch",
    )(pos3, hn, xs_init)


def _expert_body(te_ref, nu_ref, xs_ref, wg_ref, wu_ref, wd_ref, ys_ref):
    del te_ref
    i = pl.program_id(0)

    @pl.when(i < nu_ref[0])
    def _():
        x = xs_ref[...].astype(BF16)
        g = jnp.dot(x, wg_ref[...], preferred_element_type=F32)
        u = jnp.dot(x, wu_ref[...], preferred_element_type=F32)
        act = (g / (1.0 + jnp.exp(-g)) * u).astype(BF16)
        ys_ref[...] = jnp.dot(act, wd_ref[...], preferred_element_type=F32)

    @pl.when(i >= nu_ref[0])
    def _():
        ys_ref[...] = jnp.zeros(ys_ref.shape, ys_ref.dtype)


def _experts(tile_expert, n_used, xs, wg, wu, wd):
    nt = xs.shape[0] // EXPERT_TM
    grid_spec = pltpu.PrefetchScalarGridSpec(
        num_scalar_prefetch=2,
        grid=(nt,),
        in_specs=[
            pl.BlockSpec((EXPERT_TM, D_MODEL), lambda i, te, nu: (i, 0)),
            pl.BlockSpec((None, D_MODEL, D_EXPERT), lambda i, te, nu: (te[i], 0, 0)),
            pl.BlockSpec((None, D_MODEL, D_EXPERT), lambda i, te, nu: (te[i], 0, 0)),
            pl.BlockSpec((None, D_EXPERT, D_MODEL), lambda i, te, nu: (te[i], 0, 0)),
        ],
        out_specs=pl.BlockSpec((EXPERT_TM, D_MODEL), lambda i, te, nu: (i, 0)),
    )
    return pl.pallas_call(
        _expert_body,
        grid_spec=grid_spec,
        out_shape=jax.ShapeDtypeStruct(xs.shape, F32),
        compiler_params=_params(("arbitrary",)),
        name="experts",
    )(tile_expert, n_used, xs, wg, wu, wd)


def _combine_body(pos_ref, h1_ref, rw_ref, g_ref, ys_hbm, o_ref, y0_ref, y1_ref, sem):
    def issue(r, carry):
        _row_copy(ys_hbm, pos_ref[0, r], y0_ref, r, sem).start()
        _row_copy(ys_hbm, pos_ref[1, r], y1_ref, r, sem).start()
        return carry

    lax.fori_loop(0, ROW_TK, issue, 0, unroll=8)
    pltpu.make_async_copy(ys_hbm.at[pl.ds(0, ROW_TK)], y0_ref, sem).wait()
    pltpu.make_async_copy(ys_hbm.at[pl.ds(0, ROW_TK)], y1_ref, sem).wait()

    rw = rw_ref[...]
    moe = rw[:, 0:1] * y0_ref[...] + rw[:, 1:2] * y1_ref[...]
    o_ref[...] = _rms(h1_ref[...] + moe, g_ref[...])


def _combine(pos3, h1, rw, g, ys):
    nt = pos3.shape[0]
    t = h1.shape[0]
    return pl.pallas_call(
        _combine_body,
        grid=(nt,),
        in_specs=[
            pl.BlockSpec((None, 2, ROW_TK), lambda i: (i, 0, 0), memory_space=pltpu.SMEM),
            pl.BlockSpec((ROW_TK, D_MODEL), lambda i: (i, 0)),
            pl.BlockSpec((ROW_TK, LANES), lambda i: (i, 0)),
            pl.BlockSpec((1, D_MODEL), lambda i: (0, 0)),
            pl.BlockSpec(memory_space=pl.ANY),
        ],
        out_specs=pl.BlockSpec((ROW_TK, D_MODEL), lambda i: (i, 0)),
        out_shape=jax.ShapeDtypeStruct((t, D_MODEL), F32),
        scratch_shapes=[
            pltpu.VMEM((ROW_TK, D_MODEL), F32),
            pltpu.VMEM((ROW_TK, D_MODEL), F32),
            pltpu.SemaphoreType.DMA(()),
        ],
        compiler_params=_params(("arbitrary",)),
        name="combine",
    )(pos3, h1, rw, g, ys)


def _layer(x2d, meta_blk, batch, seq, attn_norm_g, w_in, b_forget, fox_out_g, ret_out_g, w_out,
           ffn_norm_g, w_rg, b_rg, w_re, b_re, w_gate, w_up, w_down, consts):
    t = batch * seq
    fw = FOX_WIDTH
    c0 = 3 * fw
    c1 = c0 + N_FOX_HEADS
    w_main = jnp.concatenate([w_in[:, :c0], w_in[:, c1:]], axis=1).astype(BF16)
    w_flog = jnp.pad(w_in[:, c0:c1], ((0, 0), (0, LANES - N_FOX_HEADS))).astype(BF16)
    g_attn = attn_norm_g.reshape(1, D_MODEL)

    u, flog = _inproj(x2d, g_attn, w_main, w_flog, INPROJ_TM)
    um, flog_m = _inproj(meta_blk, g_attn, w_main, w_flog, BLOCK)

    fl_all = jnp.concatenate(
        [jnp.broadcast_to(flog_m[None], (batch, BLOCK, LANES)), flog.reshape(batch, seq, LANES)], axis=1)
    bf_pad = jnp.pad(b_forget.reshape(1, N_FOX_HEADS), ((0, 0), (0, LANES - N_FOX_HEADS)))
    ccol, crow = _gates(fl_all, bf_pad)

    o_fox = _fox(u, um, ccol, crow, batch, seq)
    o_ret = _retention(u, um, consts["cos"], consts["sin"], consts["cosm"], consts["sinm"],
                       consts["dmask"], consts["zeta"], consts["xi"], consts["decays"],
                       ret_out_g.reshape(1, RET_WIDTH), batch, seq)

    w_router = jnp.zeros((D_MODEL, LANES), F32)
    w_router = w_router.at[:, 0:N_GROUPS].set(w_rg).at[:, 8:8 + N_EXPERTS].set(w_re)
    b_router = jnp.zeros((1, LANES), F32)
    b_router = b_router.at[0, 0:N_GROUPS].set(b_rg).at[0, 8:8 + N_EXPERTS].set(b_re)
    h1, hn2, ri, rw, cnt = _route(o_fox, o_ret, x2d, fox_out_g.reshape(1, FOX_WIDTH),
                                  w_out.astype(BF16), ffn_norm_g.reshape(1, D_MODEL), w_router, b_router)

    counts = cnt[:, 0].astype(I32)
    padded = ((counts + EXPERT_TM - 1) // EXPERT_TM) * EXPERT_TM
    ends = jnp.cumsum(padded)
    offs = ends - padded
    n_tiles = 2 * t // EXPERT_TM + N_EXPERTS
    n_used = (ends[-1] // EXPERT_TM).astype(I32)
    starts = jnp.arange(n_tiles, dtype=I32) * EXPERT_TM
    te = jnp.sum((starts[:, None] >= ends[None, :]).astype(I32), axis=1)
    last_e = jnp.max(jnp.where(padded > 0, jnp.arange(N_EXPERTS, dtype=I32), 0))
    te = jnp.minimum(te, last_e).astype(I32)
    pos = jnp.stack([offs[ri[0]] + ri[2], offs[ri[1]] + ri[3]], axis=0)
    pos3 = pos.reshape(2, t // ROW_TK, ROW_TK).transpose(1, 0, 2)

    xs = _dispatch(pos3, hn2, jnp.zeros((n_tiles * EXPERT_TM, D_MODEL), F32))
    ys = _experts(te, n_used.reshape(1), xs, w_gate.astype(BF16), w_up.astype(BF16),
                  w_down.astype(BF16))
    return pos3, h1, rw, ys


def _constants(seq):
    lg = jnp.log(1.0 - 2.0 ** (-5.0 - jnp.arange(N_RET_HEADS, dtype=F32)))
    angle = 1.0 / (RET_ROPE_BASE ** jnp.linspace(0.0, 1.0, HEAD_DIM // 2, dtype=F32))
    angle = jnp.repeat(angle, 2)
    pos = (jnp.arange(seq + BLOCK) - PAD).astype(F32)
    phase = pos[:, None] * angle[None, :]
    sin = jnp.sin(phase)
    cos = jnp.cos(phase)
    idx = jnp.arange(BLOCK, dtype=F32)
    diff = idx[:, None] - idx[None, :]
    dmask = jnp.where(diff[None] >= 0, jnp.exp(lg[:, None, None] * jnp.maximum(diff, 0.0)[None]), 0.0)
    zeta = jnp.exp(lg[:, None] * (BLOCK - 1 - idx)[None, :])
    xi = jnp.exp(lg[:, None] * (idx + 1.0)[None, :])
    bshape = (N_RET_HEADS, BLOCK, HEAD_DIM)
    return {
        "cos": cos[BLOCK:], "sin": sin[BLOCK:], "cosm": cos[:BLOCK], "sinm": sin[:BLOCK],
        "dmask": dmask,
        "zeta": jnp.broadcast_to(zeta[:, :, None], bshape),
        "xi": jnp.broadcast_to(xi[:, :, None], bshape),
        "decays": tuple(float((1.0 - 2.0 ** (-5.0 - h)) ** BLOCK) for h in range(N_RET_HEADS)),
    }


def kernel(x, meta_tokens, attn_norm_g, w_in, b_forget, fox_out_g, ret_out_g, w_out, ffn_norm_g,
           w_router_group, b_router_group, w_router_expert, b_router_expert, w_gate, w_up, w_down,
           final_norm_g):
    batch, seq, _ = x.shape
    depth = w_in.shape[0]
    assert depth == 1, "single-layer trunk"
    x2d = x.reshape(batch * seq, D_MODEL)
    meta_blk = jnp.concatenate([jnp.zeros((PAD, D_MODEL), x.dtype), meta_tokens.astype(x.dtype)], axis=0)
    consts = _constants(seq)
    pos3, h1, rw, ys = _layer(
        x2d, meta_blk, batch, seq, attn_norm_g[0], w_in[0], b_forget[0], fox_out_g[0], ret_out_g[0],
        w_out[0], ffn_norm_g[0], w_router_group[0], b_router_group[0], w_router_expert[0],
        b_router_expert[0], w_gate[0], w_up[0], w_down[0], consts)
    out = _combine(pos3, h1, rw, final_norm_g.reshape(1, D_MODEL), ys)
    return out.reshape(batch, seq, D_MODEL)
```

```python
import functools
import math

import jax
import jax.numpy as jnp
from jax import lax
from jax.experimental import pallas as pl
from jax.experimental.pallas import tpu as pltpu

D_MODEL = 2048
N_META = 16
BLOCK = 128
PAD = BLOCK - N_META
HEAD_DIM = 128
N_FOX_HEADS = 8
N_RET_HEADS = 8
FOX_WIDTH = N_FOX_HEADS * HEAD_DIM
RET_WIDTH = N_RET_HEADS * HEAD_DIM
RET_ROPE_BASE = 10000.0
N_GROUPS = 4
EXPERTS_PER_GROUP = 4
N_EXPERTS = N_GROUPS * EXPERTS_PER_GROUP
D_EXPERT = 1024
EPS = 1e-6
MASK_VALUE = -1e30
SCALE = HEAD_DIM ** -0.5
LOG2E = math.log2(math.e)

F32 = jnp.float32
BF16 = jnp.bfloat16
I32 = jnp.int32

LANES = 128
VMEM_LIMIT = 52 * 1024 * 1024

SLAB_FQ, SLAB_FK, SLAB_FV, SLAB_RQ, SLAB_RK, SLAB_RV, SLAB_RG = range(7)
N_SLABS = 7
HEADS_PER_SLAB = 8

INPROJ_TM = 1024
FOX_BQ = 512
FOX_BK = 256
ROUTE_TM = 256
ROW_TK = 256
EXPERT_TM = 256


def _params(sem):
    return pltpu.CompilerParams(dimension_semantics=sem, vmem_limit_bytes=VMEM_LIMIT)


def _rms(x, g):
    ms = jnp.mean(x * x, axis=-1, keepdims=True)
    return x * lax.rsqrt(ms + EPS) * g


def _inproj_body(x_ref, g_ref, w_ref, wf_ref, o_ref, fl_ref, vt_ref, hn_ref):
    j = pl.program_id(1)

    @pl.when(j == 0)
    def _():
        hn = _rms(x_ref[...], g_ref[...]).astype(BF16)
        hn_ref[...] = hn
        fl_ref[...] = jnp.dot(hn, wf_ref[...], preferred_element_type=F32)

    acc = jnp.dot(hn_ref[...], w_ref[...], preferred_element_type=F32)
    for hh in range(HEADS_PER_SLAB):
        o_ref[hh] = acc[:, hh * HEAD_DIM:(hh + 1) * HEAD_DIM].astype(BF16)

    @pl.when(j == SLAB_FV)
    def _():
        for hh in range(HEADS_PER_SLAB):
            vt_ref[hh] = acc[:, hh * HEAD_DIM:(hh + 1) * HEAD_DIM].T.astype(BF16)


def _inproj(x2d, g, w_main, w_flog, tm):
    t = x2d.shape[0]
    tn = HEADS_PER_SLAB * HEAD_DIM
    return pl.pallas_call(
        _inproj_body,
        grid=(t // tm, N_SLABS),
        in_specs=[
            pl.BlockSpec((tm, D_MODEL), lambda i, j: (i, 0)),
            pl.BlockSpec((1, D_MODEL), lambda i, j: (0, 0)),
            pl.BlockSpec((D_MODEL, tn), lambda i, j: (0, j)),
            pl.BlockSpec((D_MODEL, LANES), lambda i, j: (0, 0)),
        ],
        out_specs=[
            pl.BlockSpec((HEADS_PER_SLAB, tm, HEAD_DIM), lambda i, j: (j, i, 0)),
            pl.BlockSpec((tm, LANES), lambda i, j: (i, 0)),
            pl.BlockSpec((HEADS_PER_SLAB, HEAD_DIM, tm), lambda i, j: (0, 0, i)),
        ],
        out_shape=[
            jax.ShapeDtypeStruct((N_SLABS * HEADS_PER_SLAB, t, HEAD_DIM), BF16),
            jax.ShapeDtypeStruct((t, LANES), F32),
            jax.ShapeDtypeStruct((HEADS_PER_SLAB, HEAD_DIM, t), BF16),
        ],
        scratch_shapes=[pltpu.VMEM((tm, D_MODEL), BF16)],
        compiler_params=_params(("parallel", "arbitrary")),
        name="inproj",
    )(x2d, g, w_main, w_flog)


def _gates_body(fl_ref, bf_ref, tri_ref, sel_ref, crow_ref, kx_ref, carry_ref):
    c = pl.program_id(1)

    @pl.when(c == 0)
    def _():
        carry_ref[...] = jnp.zeros_like(carry_ref)

    z = fl_ref[...] + bf_ref[...]
    lf = jnp.minimum(z, 0.0) - jnp.log(1.0 + jnp.exp(-jnp.abs(z)))
    row = lax.broadcasted_iota(I32, lf.shape, 0)
    lf = jnp.where(jnp.logical_or(c > 0, row >= PAD), lf, 0.0)
    cs = jnp.dot(tri_ref[...], lf, precision=lax.Precision.HIGHEST,
                 preferred_element_type=F32) + carry_ref[...]
    carry_ref[...] = cs[BLOCK - 1:BLOCK, :]
    cs2 = cs * LOG2E
    crow_ref[...] = cs2.T[0:N_FOX_HEADS, :]
    hi = cs2.astype(BF16)
    r1 = cs2 - hi.astype(F32)
    mid = r1.astype(BF16)
    lo = (r1 - mid.astype(F32)).astype(BF16)
    kx = (jnp.dot(hi, sel_ref[0], preferred_element_type=F32)
          + jnp.dot(mid, sel_ref[1], preferred_element_type=F32)
          + jnp.dot(lo, sel_ref[2], preferred_element_type=F32))
    for hh in range(N_FOX_HEADS):
        kx_ref[hh] = kx[:, hh * LANES:(hh + 1) * LANES].astype(BF16)


def _gates(fl_all, bf_pad):
    b, p, _ = fl_all.shape
    tri = (lax.broadcasted_iota(I32, (BLOCK, BLOCK), 0)
           >= lax.broadcasted_iota(I32, (BLOCK, BLOCK), 1)).astype(F32)
    r = lax.broadcasted_iota(I32, (3, LANES, N_FOX_HEADS * LANES), 1)
    cc = lax.broadcasted_iota(I32, (3, LANES, N_FOX_HEADS * LANES), 2)
    jj = lax.broadcasted_iota(I32, (3, LANES, N_FOX_HEADS * LANES), 0)
    sel = jnp.logical_and(r < N_FOX_HEADS, cc == r * LANES + jj).astype(BF16)
    return pl.pallas_call(
        _gates_body,
        grid=(b, p // BLOCK),
        in_specs=[
            pl.BlockSpec((None, BLOCK, LANES), lambda bi, c: (bi, c, 0)),
            pl.BlockSpec((1, LANES), lambda bi, c: (0, 0)),
            pl.BlockSpec((BLOCK, BLOCK), lambda bi, c: (0, 0)),
            pl.BlockSpec((3, LANES, N_FOX_HEADS * LANES), lambda bi, c: (0, 0, 0)),
        ],
        out_specs=[
            pl.BlockSpec((None, N_FOX_HEADS, BLOCK), lambda bi, c: (bi, 0, c)),
            pl.BlockSpec((N_FOX_HEADS, BLOCK, LANES), lambda bi, c: (bi, c, 0)),
        ],
        out_shape=[
            jax.ShapeDtypeStruct((b, N_FOX_HEADS, p), F32),
            jax.ShapeDtypeStruct((b * N_FOX_HEADS, p, LANES), BF16),
        ],
        scratch_shapes=[pltpu.VMEM((1, LANES), F32)],
        compiler_params=_params(("parallel", "arbitrary")),
        name="gates",
    )(fl_all, bf_pad, tri, sel)


def _fox_body(q_ref, k_ref, kx_ref, vt_ref, km_ref, vmt_ref, crow_ref, o_ref,
              s_ref, m_ref, l_ref, acc_ref):
    i = pl.program_id(2)
    bq, bk = FOX_BQ, FOX_BK

    qs = (q_ref[...].astype(F32) * (SCALE * LOG2E)).astype(BF16)
    lane = lax.broadcasted_iota(I32, (bq, LANES), 1)
    qx = jnp.where(lane < 3, -1.0, 0.0).astype(BF16)
    qa = jnp.concatenate([qs, qx], axis=1)
    ct = crow_ref[:, pl.ds(pl.multiple_of(BLOCK + i * bq, BLOCK), bq)]

    def qk(kb, kxb):
        ka = jnp.concatenate([kb, kxb], axis=1)
        return lax.dot_general(ka, qa, (((1,), (1,)), ((), ())), preferred_element_type=F32)

    tmeta = qk(km_ref[...], kx_ref[0:BLOCK, :])
    tmeta = jnp.where(lax.broadcasted_iota(I32, (BLOCK, bq), 0) >= PAD, tmeta, MASK_VALUE)
    m0 = jnp.max(tmeta, axis=0, keepdims=True) + ct
    pm = jnp.exp2(tmeta + (ct - m0))
    m_ref[...] = m0
    l_ref[...] = jnp.sum(pm, axis=0, keepdims=True)
    acc_ref[...] = jnp.dot(vmt_ref[...], pm.astype(BF16), preferred_element_type=F32)

    def score_next(u, slot):
        off = pl.multiple_of(u * bk, bk)
        s_ref[slot] = qk(k_ref[pl.ds(off, bk), :],
                         kx_ref[pl.ds(pl.multiple_of(BLOCK + off, BLOCK), bk), :])

    def process(u, slot, shift):
        off = pl.multiple_of(u * bk, bk)
        t = s_ref[slot]
        if shift is not None:
            keep = (lax.broadcasted_iota(I32, (bk, bq), 0) + shift
                    <= lax.broadcasted_iota(I32, (bk, bq), 1))
            t = jnp.where(keep, t, MASK_VALUE)
        m_prev = m_ref[...]
        m_new = jnp.maximum(m_prev, jnp.max(t, axis=0, keepdims=True) + ct)
        alpha = jnp.exp2(m_prev - m_new)
        p = jnp.exp2(t + (ct - m_new))
        l_ref[...] = alpha * l_ref[...] + jnp.sum(p, axis=0, keepdims=True)
        acc_ref[...] = alpha * acc_ref[...] + jnp.dot(vt_ref[:, pl.ds(off, bk)], p.astype(BF16),
                                                      preferred_element_type=F32)
        m_ref[...] = m_new

    score_next(0, 0)

    def pair(jj, carry):
        u0 = 2 * jj
        score_next(u0 + 1, 1)
        process(u0, 0, None)
        score_next(u0 + 2, 0)
        process(u0 + 1, 1, None)
        return carry

    def quad(jj, carry):
        u0 = 4 * jj
        score_next(u0 + 1, 1)
        process(u0, 0, None)
        score_next(u0 + 2, 0)
        process(u0 + 1, 1, None)
        score_next(u0 + 3, 1)
        process(u0 + 2, 0, None)
        score_next(u0 + 4, 0)
        process(u0 + 3, 1, None)
        return carry

    lax.fori_loop(0, i >> 1, quad, 0)

    @pl.when((i & 1) == 1)
    def _():
        pair(i - 1, 0)

    score_next(2 * i + 1, 1)
    process(2 * i, 0, 0)
    process(2 * i + 1, 1, bk)
    o_ref[...] = (acc_ref[...] / l_ref[...]).T.astype(BF16)


def _fox(u, um, vt, vmt, kx, crow, batch, seq):
    nq = seq // FOX_BQ
    p = seq + BLOCK
    return pl.pallas_call(
        _fox_body,
        grid=(batch, N_FOX_HEADS, nq),
        in_specs=[
            pl.BlockSpec((None, FOX_BQ, HEAD_DIM), lambda b, h, i: (SLAB_FQ * 8 + h, b * nq + i, 0)),
            pl.BlockSpec((None, seq, HEAD_DIM), lambda b, h, i: (SLAB_FK * 8 + h, b, 0)),
            pl.BlockSpec((None, p, LANES), lambda b, h, i: (b * N_FOX_HEADS + h, 0, 0)),
            pl.BlockSpec((None, HEAD_DIM, seq), lambda b, h, i: (h, 0, b)),
            pl.BlockSpec((None, BLOCK, HEAD_DIM), lambda b, h, i: (SLAB_FK * 8 + h, 0, 0)),
            pl.BlockSpec((None, HEAD_DIM, BLOCK), lambda b, h, i: (h, 0, 0)),
            pl.BlockSpec((None, 1, p), lambda b, h, i: (b * N_FOX_HEADS + h, 0, 0)),
        ],
        out_specs=pl.BlockSpec((None, FOX_BQ, HEAD_DIM), lambda b, h, i: (h, b * nq + i, 0)),
        out_shape=jax.ShapeDtypeStruct((N_FOX_HEADS, batch * seq, HEAD_DIM), BF16),
        scratch_shapes=[
            pltpu.VMEM((2, FOX_BK, FOX_BQ), F32),
            pltpu.VMEM((1, FOX_BQ), F32),
            pltpu.VMEM((1, FOX_BQ), F32),
            pltpu.VMEM((HEAD_DIM, FOX_BQ), F32),
        ],
        compiler_params=_params(("parallel", "parallel", "parallel")),
        name="fox",
    )(u, u, kx, vt, um, vmt, crow.reshape(batch * N_FOX_HEADS, 1, p))


def _ret_body(decays, rq_ref, rk_ref, rv_ref, rg_ref, mk_ref, mv_ref, cos_ref, sin_ref,
              cosm_ref, sinm_ref, dm_ref, zt_ref, xi_ref, gn_ref, o_ref, s_ref):
    n = pl.program_id(1)
    lane = lax.broadcasted_iota(I32, (BLOCK, HEAD_DIM), 1)
    even = (lane & 1) == 0

    def rope(x, cos, sin):
        rot = jnp.where(even, -pltpu.roll(x, HEAD_DIM - 1, 1), pltpu.roll(x, 1, 1))
        return x * cos + rot * sin

    @pl.when(n == 0)
    def _():
        rowm = lax.broadcasted_iota(I32, (BLOCK, HEAD_DIM), 0)
        for hh in range(N_RET_HEADS):
            km = rope(mk_ref[hh].astype(F32) * SCALE, cosm_ref[...], sinm_ref[...])
            km = jnp.where(rowm >= PAD, km, 0.0)
            kz = (km * zt_ref[hh]).T.astype(BF16)
            s_ref[hh] = jnp.dot(kz, mv_ref[hh], preferred_element_type=F32)

    cos = cos_ref[...]
    sin = sin_ref[...]
    for hh in range(N_RET_HEADS):
        qr = rope(rq_ref[hh].astype(F32), cos, sin)
        kr = rope(rk_ref[hh].astype(F32) * SCALE, cos, sin)
        v = rv_ref[hh]
        sc = lax.dot_general(qr.astype(BF16), kr.astype(BF16), (((1,), (1,)), ((), ())),
                             preferred_element_type=F32) * dm_ref[hh]
        inner = jnp.dot(sc.astype(BF16), v, preferred_element_type=F32)
        s_prev = s_ref[hh]
        cross = jnp.dot((qr * xi_ref[hh]).astype(BF16), s_prev.astype(BF16),
                        preferred_element_type=F32)
        kz = (kr * zt_ref[hh]).T.astype(BF16)
        s_ref[hh] = s_prev * decays[hh] + jnp.dot(kz, v, preferred_element_type=F32)

        o = inner + cross
        mu = jnp.mean(o, axis=-1, keepdims=True)
        d = o - mu
        var = jnp.mean(d * d, axis=-1, keepdims=True)
        on = d * lax.rsqrt(var + EPS) * gn_ref[:, hh * HEAD_DIM:(hh + 1) * HEAD_DIM]
        g = rg_ref[hh].astype(F32)
        silu = g / (1.0 + jnp.exp(-g))
        o_ref[:, hh * HEAD_DIM:(hh + 1) * HEAD_DIM] = (on * silu).astype(BF16)


def _retention(u, um, cos, sin, cosm, sinm, dmask, zeta, xi, decays, gain, batch, seq):
    nc = seq // BLOCK
    hs = (N_RET_HEADS, BLOCK, HEAD_DIM)
    full3 = lambda b, n: (0, 0, 0)
    return pl.pallas_call(
        functools.partial(_ret_body, decays),
        grid=(batch, nc),
        in_specs=[
            pl.BlockSpec(hs, lambda b, n: (SLAB_RQ, b * nc + n, 0)),
            pl.BlockSpec(hs, lambda b, n: (SLAB_RK, b * nc + n, 0)),
            pl.BlockSpec(hs, lambda b, n: (SLAB_RV, b * nc + n, 0)),
            pl.BlockSpec(hs, lambda b, n: (SLAB_RG, b * nc + n, 0)),
            pl.BlockSpec(hs, lambda b, n: (SLAB_RK, 0, 0)),
            pl.BlockSpec(hs, lambda b, n: (SLAB_RV, 0, 0)),
            pl.BlockSpec((BLOCK, HEAD_DIM), lambda b, n: (n, 0)),
            pl.BlockSpec((BLOCK, HEAD_DIM), lambda b, n: (n, 0)),
            pl.BlockSpec((BLOCK, HEAD_DIM), lambda b, n: (0, 0)),
            pl.BlockSpec((BLOCK, HEAD_DIM), lambda b, n: (0, 0)),
            pl.BlockSpec(hs, full3),
            pl.BlockSpec(hs, full3),
            pl.BlockSpec(hs, full3),
            pl.BlockSpec((1, RET_WIDTH), lambda b, n: (0, 0)),
        ],
        out_specs=pl.BlockSpec((BLOCK, RET_WIDTH), lambda b, n: (b * nc + n, 0)),
        out_shape=jax.ShapeDtypeStruct((batch * seq, RET_WIDTH), BF16),
        scratch_shapes=[pltpu.VMEM(hs, F32)],
        compiler_params=_params(("parallel", "arbitrary")),
        name="retention",
    )(u, u, u, u, um, um, cos, sin, cosm, sinm, dmask, zeta, xi, gain)


def _route_body(of_ref, or_ref, x_ref, fg_ref, wo_ref, ng_ref, wr_ref, br_ref, ust_ref,
                h1_ref, hn_ref, ri_ref, rw_ref, cnt_ref, run_ref):
    tm = ROUTE_TM

    @pl.when(pl.program_id(0) == 0)
    def _():
        run_ref[...] = jnp.zeros_like(run_ref)

    of = jnp.concatenate([of_ref[hh].astype(F32) for hh in range(N_FOX_HEADS)], axis=-1)
    ofn = _rms(of, fg_ref[...]).astype(BF16)
    attn = jnp.dot(ofn, wo_ref[0:FOX_WIDTH, :], preferred_element_type=F32)
    attn = attn + jnp.dot(or_ref[...], wo_ref[FOX_WIDTH:, :], preferred_element_type=F32)
    h1 = x_ref[...] + attn
    h1_ref[...] = h1
    hn = _rms(h1, ng_ref[...])
    hn_ref[...] = hn

    hi = hn.astype(BF16)
    lo = (hn - hi.astype(F32)).astype(BF16)
    la = jnp.dot(hi, wr_ref[...], preferred_element_type=F32)
    lb = jnp.dot(lo, wr_ref[...], preferred_element_type=F32)
    logits = la[:, :LANES] + la[:, LANES:] + lb[:, :LANES] + br_ref[...]
    lt = logits.T
    gl = lt[0:N_GROUPS, :]
    el = lt[8:8 + N_EXPERTS, :]

    grow = lax.broadcasted_iota(I32, (N_GROUPS, tm), 0).astype(F32)
    gmax = jnp.max(gl, axis=0, keepdims=True)
    g_star = jnp.min(jnp.where(gl == gmax, grow, float(N_GROUPS)), axis=0, keepdims=True)
    p_group = 1.0 / jnp.sum(jnp.exp(gl - gmax), axis=0, keepdims=True)

    erow_i = lax.broadcasted_iota(I32, (N_EXPERTS, tm), 0)
    erow = erow_i.astype(F32)
    egrp = (erow_i >> 2).astype(F32)
    neg = float("-inf")
    elm = jnp.where(egrp == g_star, el, neg)
    v0 = jnp.max(elm, axis=0, keepdims=True)
    i0 = jnp.min(jnp.where(elm == v0, erow, float(N_EXPERTS)), axis=0, keepdims=True)
    elm2 = jnp.where(erow == i0, neg, elm)
    v1 = jnp.max(elm2, axis=0, keepdims=True)
    i1 = jnp.min(jnp.where(elm2 == v1, erow, float(N_EXPERTS)), axis=0, keepdims=True)
    e10 = jnp.exp(v1 - v0)
    w0 = p_group / (1.0 + e10)
    w1 = p_group * e10 / (1.0 + e10)

    oh0 = erow == i0
    oh1 = erow == i1
    both = jnp.logical_or(oh0, oh1)
    prefix = jnp.dot(both.astype(BF16), ust_ref[...], preferred_element_type=F32)
    base = run_ref[...] + prefix
    rank0 = jnp.sum(jnp.where(oh0, base, 0.0), axis=0, keepdims=True)
    rank1 = jnp.sum(jnp.where(oh1, base, 0.0), axis=0, keepdims=True)
    run_new = run_ref[...] + jnp.sum(both.astype(F32), axis=1, keepdims=True)
    run_ref[...] = run_new
    cnt_ref[...] = run_new

    r8 = lax.broadcasted_iota(I32, (8, tm), 0)
    ri = jnp.where(r8 == 0, i0, jnp.where(r8 == 1, i1, jnp.where(r8 == 2, rank0,
                                                                  jnp.where(r8 == 3, rank1, 0.0))))
    ri_ref[...] = ri.astype(I32)
    r128 = lax.broadcasted_iota(I32, (LANES, tm), 0)
    rw = jnp.where(r128 == 0, w0, jnp.where(r128 == 1, w1, 0.0))
    rw_ref[...] = rw.T


def _route(o_fox, o_ret, x2d, fox_g, w_out, ffn_g, w_router, b_router):
    t = x2d.shape[0]
    tm = ROUTE_TM
    ust = (lax.broadcasted_iota(I32, (tm, tm), 0) < lax.broadcasted_iota(I32, (tm, tm), 1)).astype(BF16)
    const2 = lambda i: (0, 0)
    return pl.pallas_call(
        _route_body,
        grid=(t // tm,),
        in_specs=[
            pl.BlockSpec((N_FOX_HEADS, tm, HEAD_DIM), lambda i: (0, i, 0)),
            pl.BlockSpec((tm, RET_WIDTH), lambda i: (i, 0)),
            pl.BlockSpec((tm, D_MODEL), lambda i: (i, 0)),
            pl.BlockSpec((1, FOX_WIDTH), const2),
            pl.BlockSpec((FOX_WIDTH + RET_WIDTH, D_MODEL), const2),
            pl.BlockSpec((1, D_MODEL), const2),
            pl.BlockSpec((D_MODEL, 2 * LANES), const2),
            pl.BlockSpec((1, LANES), const2),
            pl.BlockSpec((tm, tm), const2),
        ],
        out_specs=[
            pl.BlockSpec((tm, D_MODEL), lambda i: (i, 0)),
            pl.BlockSpec((tm, D_MODEL), lambda i: (i, 0)),
            pl.BlockSpec((8, tm), lambda i: (0, i)),
            pl.BlockSpec((tm, LANES), lambda i: (i, 0)),
            pl.BlockSpec((N_EXPERTS, tm), const2),
        ],
        out_shape=[
            jax.ShapeDtypeStruct((t, D_MODEL), F32),
            jax.ShapeDtypeStruct((t, D_MODEL), F32),
            jax.ShapeDtypeStruct((8, t), I32),
            jax.ShapeDtypeStruct((t, LANES), F32),
            jax.ShapeDtypeStruct((N_EXPERTS, tm), F32),
        ],
        scratch_shapes=[pltpu.VMEM((N_EXPERTS, tm), F32)],
        compiler_params=_params(("arbitrary",)),
        name="route",
    )(o_fox, o_ret, x2d, fox_g, w_out, ffn_g, w_router, b_router, ust)


def _row_copy(src, src_row, dst, dst_row, sem):
    return pltpu.make_async_copy(src.at[pl.ds(src_row, 1)], dst.at[pl.ds(dst_row, 1)], sem)


def _dispatch_body(pos_ref, hn_ref, xs_init_hbm, xs_hbm, sem):
    del xs_init_hbm

    def issue(r, carry):
        _row_copy(hn_ref, r, xs_hbm, pos_ref[0, r], sem).start()
        _row_copy(hn_ref, r, xs_hbm, pos_ref[1, r], sem).start()
        return carry

    lax.fori_loop(0, ROW_TK, issue, 0, unroll=8)
    for _ in range(2):
        pltpu.make_async_copy(hn_ref, xs_hbm.at[pl.ds(0, ROW_TK)], sem).wait()


def _dispatch(pos3, hn, xs_init):
    nt = pos3.shape[0]
    return pl.pallas_call(
        _dispatch_body,
        grid=(nt,),
        in_specs=[
            pl.BlockSpec((None, 2, ROW_TK), lambda i: (i, 0, 0), memory_space=pltpu.SMEM),
            pl.BlockSpec((ROW_TK, D_MODEL), lambda i: (i, 0)),
            pl.BlockSpec(memory_space=pl.ANY),
        ],
        out_specs=pl.BlockSpec(memory_space=pl.ANY),
        out_shape=jax.ShapeDtypeStruct(xs_init.shape, xs_init.dtype),
        scratch_shapes=[pltpu.SemaphoreType.DMA(())],
        input_output_aliases={2: 0},
        compiler_params=_params(("arbitrary",)),
        name="dispatch",
    )(pos3, hn, xs_init)


def _expert_body(te_ref, nu_ref, xs_ref, wg_ref, wu_ref, wd_ref, ys_ref):
    del te_ref
    i = pl.program_id(0)

    @pl.when(i < nu_ref[0])
    def _():
        x = xs_ref[...].astype(BF16)
        g = jnp.dot(x, wg_ref[...], preferred_element_type=F32)
        u = jnp.dot(x, wu_ref[...], preferred_element_type=F32)
        act = (g / (1.0 + jnp.exp(-g)) * u).astype(BF16)
        ys_ref[...] = jnp.dot(act, wd_ref[...], preferred_element_type=F32)

    @pl.when(i >= nu_ref[0])
    def _():
        ys_ref[...] = jnp.zeros(ys_ref.shape, ys_ref.dtype)


def _experts(tile_expert, n_used, xs, wg, wu, wd):
    nt = xs.shape[0] // EXPERT_TM
    grid_spec = pltpu.PrefetchScalarGridSpec(
        num_scalar_prefetch=2,
        grid=(nt,),
        in_specs=[
            pl.BlockSpec((EXPERT_TM, D_MODEL), lambda i, te, nu: (i, 0)),
            pl.BlockSpec((None, D_MODEL, D_EXPERT), lambda i, te, nu: (te[i], 0, 0)),
            pl.BlockSpec((None, D_MODEL, D_EXPERT), lambda i, te, nu: (te[i], 0, 0)),
            pl.BlockSpec((None, D_EXPERT, D_MODEL), lambda i, te, nu: (te[i], 0, 0)),
        ],
        out_specs=pl.BlockSpec((EXPERT_TM, D_MODEL), lambda i, te, nu: (i, 0)),
    )
    return pl.pallas_call(
        _expert_body,
        grid_spec=grid_spec,
        out_shape=jax.ShapeDtypeStruct(xs.shape, F32),
        compiler_params=_params(("arbitrary",)),
        name="experts",
    )(tile_expert, n_used, xs, wg, wu, wd)


def _combine_body(pos_ref, h1_ref, rw_ref, g_ref, ys_hbm, o_ref, y0_ref, y1_ref, sem):
    def issue(r, carry):
        _row_copy(ys_hbm, pos_ref[0, r], y0_ref, r, sem).start()
        _row_copy(ys_hbm, pos_ref[1, r], y1_ref, r, sem).start()
        return carry

    lax.fori_loop(0, ROW_TK, issue, 0, unroll=8)
    pltpu.make_async_copy(ys_hbm.at[pl.ds(0, ROW_TK)], y0_ref, sem).wait()
    pltpu.make_async_copy(ys_hbm.at[pl.ds(0, ROW_TK)], y1_ref, sem).wait()

    rw = rw_ref[...]
    moe = rw[:, 0:1] * y0_ref[...] + rw[:, 1:2] * y1_ref[...]
    o_ref[...] = _rms(h1_ref[...] + moe, g_ref[...])


def _combine(pos3, h1, rw, g, ys):
    nt = pos3.shape[0]
    t = h1.shape[0]
    return pl.pallas_call(
        _combine_body,
        grid=(nt,),
        in_specs=[
            pl.BlockSpec((None, 2, ROW_TK), lambda i: (i, 0, 0), memory_space=pltpu.SMEM),
            pl.BlockSpec((ROW_TK, D_MODEL), lambda i: (i, 0)),
            pl.BlockSpec((ROW_TK, LANES), lambda i: (i, 0)),
            pl.BlockSpec((1, D_MODEL), lambda i: (0, 0)),
            pl.BlockSpec(memory_space=pl.ANY),
        ],
        out_specs=pl.BlockSpec((ROW_TK, D_MODEL), lambda i: (i, 0)),
        out_shape=jax.ShapeDtypeStruct((t, D_MODEL), F32),
        scratch_shapes=[
            pltpu.VMEM((ROW_TK, D_MODEL), F32),
            pltpu.VMEM((ROW_TK, D_MODEL), F32),
            pltpu.SemaphoreType.DMA(()),
        ],
        compiler_params=_params(("arbitrary",)),
        name="combine",
    )(pos3, h1, rw, g, ys)


def _layer(x2d, meta_blk, batch, seq, attn_norm_g, w_in, b_forget, fox_out_g, ret_out_g, w_out,
           ffn_norm_g, w_rg, b_rg, w_re, b_re, w_gate, w_up, w_down, consts):
    t = batch * seq
    fw = FOX_WIDTH
    c0 = 3 * fw
    c1 = c0 + N_FOX_HEADS
    w_main = jnp.concatenate([w_in[:, :c0], w_in[:, c1:]], axis=1).astype(BF16)
    w_flog = jnp.pad(w_in[:, c0:c1], ((0, 0), (0, LANES - N_FOX_HEADS))).astype(BF16)
    g_attn = attn_norm_g.reshape(1, D_MODEL)

    u, flog, vt = _inproj(x2d, g_attn, w_main, w_flog, INPROJ_TM)
    um, flog_m, vmt = _inproj(meta_blk, g_attn, w_main, w_flog, BLOCK)

    fl_all = jnp.concatenate(
        [jnp.broadcast_to(flog_m[None], (batch, BLOCK, LANES)), flog.reshape(batch, seq, LANES)], axis=1)
    bf_pad = jnp.pad(b_forget.reshape(1, N_FOX_HEADS), ((0, 0), (0, LANES - N_FOX_HEADS)))
    crow, kx = _gates(fl_all, bf_pad)

    o_fox = _fox(u, um, vt, vmt, kx, crow, batch, seq)
    o_ret = _retention(u, um, consts["cos"], consts["sin"], consts["cosm"], consts["sinm"],
                       consts["dmask"], consts["zeta"], consts["xi"], consts["decays"],
                       ret_out_g.reshape(1, RET_WIDTH), batch, seq)

    w_router = jnp.zeros((D_MODEL, LANES), F32)
    w_router = w_router.at[:, 0:N_GROUPS].set(w_rg).at[:, 8:8 + N_EXPERTS].set(w_re)
    b_router = jnp.zeros((1, LANES), F32)
    b_router = b_router.at[0, 0:N_GROUPS].set(b_rg).at[0, 8:8 + N_EXPERTS].set(b_re)
    wr_hi = w_router.astype(BF16)
    wr_lo = (w_router - wr_hi.astype(F32)).astype(BF16)
    h1, hn2, ri, rw, cnt = _route(o_fox, o_ret, x2d, fox_out_g.reshape(1, FOX_WIDTH),
                                  w_out.astype(BF16), ffn_norm_g.reshape(1, D_MODEL),
                                  jnp.concatenate([wr_hi, wr_lo], axis=1), b_router)

    counts = cnt[:, 0].astype(I32)
    padded = ((counts + EXPERT_TM - 1) // EXPERT_TM) * EXPERT_TM
    ends = jnp.cumsum(padded)
    offs = ends - padded
    n_tiles = 2 * t // EXPERT_TM + N_EXPERTS
    n_used = (ends[-1] // EXPERT_TM).astype(I32)
    starts = jnp.arange(n_tiles, dtype=I32) * EXPERT_TM
    te = jnp.sum((starts[:, None] >= ends[None, :]).astype(I32), axis=1)
    last_e = jnp.max(jnp.where(padded > 0, jnp.arange(N_EXPERTS, dtype=I32), 0))
    te = jnp.minimum(te, last_e).astype(I32)
    pos = jnp.stack([offs[ri[0]] + ri[2], offs[ri[1]] + ri[3]], axis=0)
    pos3 = pos.reshape(2, t // ROW_TK, ROW_TK).transpose(1, 0, 2)

    xs = _dispatch(pos3, hn2, jnp.zeros((n_tiles * EXPERT_TM, D_MODEL), F32))
    ys = _experts(te, n_used.reshape(1), xs, w_gate.astype(BF16), w_up.astype(BF16),
                  w_down.astype(BF16))
    return pos3, h1, rw, ys


def _constants(seq):
    lg = jnp.log(1.0 - 2.0 ** (-5.0 - jnp.arange(N_RET_HEADS, dtype=F32)))
    angle = 1.0 / (RET_ROPE_BASE ** jnp.linspace(0.0, 1.0, HEAD_DIM // 2, dtype=F32))
    angle = jnp.repeat(angle, 2)
    pos = (jnp.arange(seq + BLOCK) - PAD).astype(F32)
    phase = pos[:, None] * angle[None, :]
    sin = jnp.sin(phase)
    cos = jnp.cos(phase)
    idx = jnp.arange(BLOCK, dtype=F32)
    diff = idx[:, None] - idx[None, :]
    dmask = jnp.where(diff[None] >= 0, jnp.exp(lg[:, None, None] * jnp.maximum(diff, 0.0)[None]), 0.0)
    zeta = jnp.exp(lg[:, None] * (BLOCK - 1 - idx)[None, :])
    xi = jnp.exp(lg[:, None] * (idx + 1.0)[None, :])
    bshape = (N_RET_HEADS, BLOCK, HEAD_DIM)
    return {
        "cos": cos[BLOCK:], "sin": sin[BLOCK:], "cosm": cos[:BLOCK], "sinm": sin[:BLOCK],
        "dmask": dmask,
        "zeta": jnp.broadcast_to(zeta[:, :, None], bshape),
        "xi": jnp.broadcast_to(xi[:, :, None], bshape),
        "decays": tuple(float((1.0 - 2.0 ** (-5.0 - h)) ** BLOCK) for h in range(N_RET_HEADS)),
    }


def kernel(x, meta_tokens, attn_norm_g, w_in, b_forget, fox_out_g, ret_out_g, w_out, ffn_norm_g,
           w_router_group, b_router_group, w_router_expert, b_router_expert, w_gate, w_up, w_down,
           final_norm_g):
    batch, seq, _ = x.shape
    depth = w_in.shape[0]
    assert depth == 1, "single-layer trunk"
    x2d = x.reshape(batch * seq, D_MODEL)
    meta_blk = jnp.concatenate([jnp.zeros((PAD, D_MODEL), x.dtype), meta_tokens.astype(x.dtype)], axis=0)
    consts = _constants(seq)
    pos3, h1, rw, ys = _layer(
        x2d, meta_blk, batch, seq, attn_norm_g[0], w_in[0], b_forget[0], fox_out_g[0], ret_out_g[0],
        w_out[0], ffn_norm_g[0], w_router_group[0], b_router_group[0], w_router_expert[0],
        b_router_expert[0], w_gate[0], w_up[0], w_down[0], consts)
    out = _combine(pos3, h1, rw, final_norm_g.reshape(1, D_MODEL), ys)
    return out.reshape(batch, seq, D_MODEL)
```

```python
import functools
import math

import jax
import jax.numpy as jnp
from jax import lax
from jax.experimental import pallas as pl
from jax.experimental.pallas import tpu as pltpu

D_MODEL = 2048
N_META = 16
BLOCK = 128
PAD = BLOCK - N_META
HEAD_DIM = 128
N_FOX_HEADS = 8
N_RET_HEADS = 8
FOX_WIDTH = N_FOX_HEADS * HEAD_DIM
RET_WIDTH = N_RET_HEADS * HEAD_DIM
RET_ROPE_BASE = 10000.0
N_GROUPS = 4
EXPERTS_PER_GROUP = 4
N_EXPERTS = N_GROUPS * EXPERTS_PER_GROUP
D_EXPERT = 1024
EPS = 1e-6
MASK_VALUE = -1e30
SCALE = HEAD_DIM ** -0.5
LOG2E = math.log2(math.e)

F32 = jnp.float32
BF16 = jnp.bfloat16
I32 = jnp.int32

LANES = 128
VMEM_LIMIT = 52 * 1024 * 1024

SLAB_FQ, SLAB_FK, SLAB_FV, SLAB_RQ, SLAB_RK, SLAB_RV, SLAB_RG = range(7)
N_SLABS = 7
HEADS_PER_SLAB = 8

INPROJ_TM = 1024
FOX_BQ = 512
FOX_BK = 256
ROUTE_TM = 512
ROW_TK = 256
EXPERT_TM = 256
GATES_CHUNKS = 11


def _params(sem):
    return pltpu.CompilerParams(dimension_semantics=sem, vmem_limit_bytes=VMEM_LIMIT)


def _rms(x, g):
    ms = jnp.mean(x * x, axis=-1, keepdims=True)
    return x * lax.rsqrt(ms + EPS) * g


def _inproj_body(x_ref, g_ref, w_ref, wf_ref, o_ref, fl_ref, vt_ref, hn_ref):
    j = pl.program_id(1)

    @pl.when(j == 0)
    def _():
        hn = _rms(x_ref[...], g_ref[...]).astype(BF16)
        hn_ref[...] = hn
        fl_ref[...] = jnp.dot(hn, wf_ref[...], preferred_element_type=F32)

    acc = jnp.dot(hn_ref[...], w_ref[...], preferred_element_type=F32)
    for hh in range(HEADS_PER_SLAB):
        o_ref[hh] = acc[:, hh * HEAD_DIM:(hh + 1) * HEAD_DIM].astype(BF16)

    @pl.when(j == SLAB_FV)
    def _():
        for hh in range(HEADS_PER_SLAB):
            vt_ref[hh] = acc[:, hh * HEAD_DIM:(hh + 1) * HEAD_DIM].T.astype(BF16)


def _inproj(x2d, g, w_main, w_flog, tm):
    t = x2d.shape[0]
    tn = HEADS_PER_SLAB * HEAD_DIM
    return pl.pallas_call(
        _inproj_body,
        grid=(t // tm, N_SLABS),
        in_specs=[
            pl.BlockSpec((tm, D_MODEL), lambda i, j: (i, 0)),
            pl.BlockSpec((1, D_MODEL), lambda i, j: (0, 0)),
            pl.BlockSpec((D_MODEL, tn), lambda i, j: (0, j)),
            pl.BlockSpec((D_MODEL, LANES), lambda i, j: (0, 0)),
        ],
        out_specs=[
            pl.BlockSpec((HEADS_PER_SLAB, tm, HEAD_DIM), lambda i, j: (j, i, 0)),
            pl.BlockSpec((tm, LANES), lambda i, j: (i, 0)),
            pl.BlockSpec((HEADS_PER_SLAB, HEAD_DIM, tm), lambda i, j: (0, 0, i)),
        ],
        out_shape=[
            jax.ShapeDtypeStruct((N_SLABS * HEADS_PER_SLAB, t, HEAD_DIM), BF16),
            jax.ShapeDtypeStruct((t, LANES), F32),
            jax.ShapeDtypeStruct((HEADS_PER_SLAB, HEAD_DIM, t), BF16),
        ],
        scratch_shapes=[pltpu.VMEM((tm, D_MODEL), BF16)],
        compiler_params=_params(("parallel", "arbitrary")),
        name="inproj",
    )(x2d, g, w_main, w_flog)


def _split3(x):
    hi = x.astype(BF16)
    r1 = x - hi.astype(F32)
    mid = r1.astype(BF16)
    lo = (r1 - mid.astype(F32)).astype(BF16)
    return hi, mid, lo


def _gates_body(fl_ref, bf_ref, tri_ref, sel_ref, crow_ref, kx_ref, carry_ref):
    c = pl.program_id(1)

    @pl.when(c == 0)
    def _():
        carry_ref[...] = jnp.zeros_like(carry_ref)

    carry = carry_ref[...]
    lane = lax.broadcasted_iota(I32, (BLOCK, LANES), 1)
    for g in range(GATES_CHUNKS):
        rows = slice(g * BLOCK, (g + 1) * BLOCK)
        z = fl_ref[rows, :] + bf_ref[...]
        lf = jnp.minimum(z, 0.0) - jnp.log(1.0 + jnp.exp(-jnp.abs(z)))
        if g == 0:
            row = lax.broadcasted_iota(I32, lf.shape, 0)
            lf = jnp.where(jnp.logical_or(c > 0, row >= PAD), lf, 0.0)
        lf = jnp.where(lane < N_FOX_HEADS, lf, 0.0)
        cs = jnp.dot(tri_ref[...], jnp.concatenate(_split3(lf), axis=1), preferred_element_type=F32)
        cs = cs[:, :LANES] + cs[:, LANES:2 * LANES] + cs[:, 2 * LANES:] + carry
        carry = cs[BLOCK - 1:BLOCK, :]
        cs2 = cs * LOG2E
        crow_ref[:, rows] = cs2.T[0:N_FOX_HEADS, :]
        hi, mid, lo = _split3(cs2)
        packed = (hi.astype(F32) + pltpu.roll(mid.astype(F32), N_FOX_HEADS, 1)
                  + pltpu.roll(lo.astype(F32), 2 * N_FOX_HEADS, 1)).astype(BF16)
        kx = jnp.dot(packed, sel_ref[...], preferred_element_type=F32)
        for hh in range(N_FOX_HEADS):
            kx_ref[hh, rows, :] = kx[:, hh * LANES:(hh + 1) * LANES].astype(BF16)
    carry_ref[...] = carry


def _gates(fl_all, bf_pad):
    b, p, _ = fl_all.shape
    rows = GATES_CHUNKS * BLOCK
    tri = (lax.broadcasted_iota(I32, (BLOCK, BLOCK), 0)
           >= lax.broadcasted_iota(I32, (BLOCK, BLOCK), 1)).astype(BF16)
    r = lax.broadcasted_iota(I32, (LANES, N_FOX_HEADS * LANES), 0)
    cc = lax.broadcasted_iota(I32, (LANES, N_FOX_HEADS * LANES), 1)
    sel = jnp.logical_and(r < 3 * N_FOX_HEADS,
                          cc == (r % N_FOX_HEADS) * LANES + r // N_FOX_HEADS).astype(BF16)
    return pl.pallas_call(
        _gates_body,
        grid=(b, p // rows),
        in_specs=[
            pl.BlockSpec((None, rows, LANES), lambda bi, c: (bi, c, 0)),
            pl.BlockSpec((1, LANES), lambda bi, c: (0, 0)),
            pl.BlockSpec((BLOCK, BLOCK), lambda bi, c: (0, 0)),
            pl.BlockSpec((LANES, N_FOX_HEADS * LANES), lambda bi, c: (0, 0)),
        ],
        out_specs=[
            pl.BlockSpec((None, N_FOX_HEADS, rows), lambda bi, c: (bi, 0, c)),
            pl.BlockSpec((N_FOX_HEADS, rows, LANES), lambda bi, c: (bi, c, 0)),
        ],
        out_shape=[
            jax.ShapeDtypeStruct((b, N_FOX_HEADS, p), F32),
            jax.ShapeDtypeStruct((b * N_FOX_HEADS, p, LANES), BF16),
        ],
        scratch_shapes=[pltpu.VMEM((1, LANES), F32)],
        compiler_params=_params(("parallel", "arbitrary")),
        name="gates",
    )(fl_all, bf_pad, tri, sel)


def _fox_body(q_ref, k_ref, kx_ref, vt_ref, km_ref, vmt_ref, crow_ref, o_ref,
              s_ref, m_ref, l_ref, acc_ref):
    i = pl.program_id(2)
    bq, bk = FOX_BQ, FOX_BK

    qs = (q_ref[...].astype(F32) * (SCALE * LOG2E)).astype(BF16)
    lane = lax.broadcasted_iota(I32, (bq, LANES), 1)
    qx = jnp.where(lane < 3, -1.0, 0.0).astype(BF16)
    qa = jnp.concatenate([qs, qx], axis=1)
    ct = crow_ref[:, pl.ds(pl.multiple_of(BLOCK + i * bq, BLOCK), bq)]

    def qk(kb, kxb):
        ka = jnp.concatenate([kb, kxb], axis=1)
        return lax.dot_general(ka, qa, (((1,), (1,)), ((), ())), preferred_element_type=F32)

    tmeta = qk(km_ref[...], kx_ref[0:BLOCK, :])
    tmeta = jnp.where(lax.broadcasted_iota(I32, (BLOCK, bq), 0) >= PAD, tmeta, MASK_VALUE)
    m0 = jnp.max(tmeta, axis=0, keepdims=True) + ct
    pm = jnp.exp2(tmeta + (ct - m0))
    m_ref[...] = m0
    l_ref[...] = jnp.sum(pm, axis=0, keepdims=True)
    acc_ref[...] = jnp.dot(vmt_ref[...], pm.astype(BF16), preferred_element_type=F32)

    def score_next(u, slot):
        off = pl.multiple_of(u * bk, bk)
        s_ref[slot] = qk(k_ref[pl.ds(off, bk), :],
                         kx_ref[pl.ds(pl.multiple_of(BLOCK + off, BLOCK), bk), :])

    def process(u, slot, shift):
        off = pl.multiple_of(u * bk, bk)
        t = s_ref[slot]
        if shift is not None:
            keep = (lax.broadcasted_iota(I32, (bk, bq), 0) + shift
                    <= lax.broadcasted_iota(I32, (bk, bq), 1))
            t = jnp.where(keep, t, MASK_VALUE)
        m_prev = m_ref[...]
        m_new = jnp.maximum(m_prev, jnp.max(t, axis=0, keepdims=True) + ct)
        alpha = jnp.exp2(m_prev - m_new)
        p = jnp.exp2(t + (ct - m_new))
        l_ref[...] = alpha * l_ref[...] + jnp.sum(p, axis=0, keepdims=True)
        acc_ref[...] = alpha * acc_ref[...] + jnp.dot(vt_ref[:, pl.ds(off, bk)], p.astype(BF16),
                                                      preferred_element_type=F32)
        m_ref[...] = m_new

    score_next(0, 0)

    def pair(jj, carry):
        u0 = 2 * jj
        score_next(u0 + 1, 1)
        process(u0, 0, None)
        score_next(u0 + 2, 0)
        process(u0 + 1, 1, None)
        return carry

    def quad(jj, carry):
        u0 = 4 * jj
        score_next(u0 + 1, 1)
        process(u0, 0, None)
        score_next(u0 + 2, 0)
        process(u0 + 1, 1, None)
        score_next(u0 + 3, 1)
        process(u0 + 2, 0, None)
        score_next(u0 + 4, 0)
        process(u0 + 3, 1, None)
        return carry

    lax.fori_loop(0, i >> 1, quad, 0)

    @pl.when((i & 1) == 1)
    def _():
        pair(i - 1, 0)

    score_next(2 * i + 1, 1)
    process(2 * i, 0, 0)
    process(2 * i + 1, 1, bk)
    o_ref[...] = (acc_ref[...] / l_ref[...]).T.astype(BF16)


def _fox(u, um, vt, vmt, kx, crow, batch, seq):
    nq = seq // FOX_BQ
    p = seq + BLOCK
    return pl.pallas_call(
        _fox_body,
        grid=(batch, N_FOX_HEADS, nq),
        in_specs=[
            pl.BlockSpec((None, FOX_BQ, HEAD_DIM), lambda b, h, i: (SLAB_FQ * 8 + h, b * nq + i, 0)),
            pl.BlockSpec((None, seq, HEAD_DIM), lambda b, h, i: (SLAB_FK * 8 + h, b, 0)),
            pl.BlockSpec((None, p, LANES), lambda b, h, i: (b * N_FOX_HEADS + h, 0, 0)),
            pl.BlockSpec((None, HEAD_DIM, seq), lambda b, h, i: (h, 0, b)),
            pl.BlockSpec((None, BLOCK, HEAD_DIM), lambda b, h, i: (SLAB_FK * 8 + h, 0, 0)),
            pl.BlockSpec((None, HEAD_DIM, BLOCK), lambda b, h, i: (h, 0, 0)),
            pl.BlockSpec((None, 1, p), lambda b, h, i: (b * N_FOX_HEADS + h, 0, 0)),
        ],
        out_specs=pl.BlockSpec((None, FOX_BQ, HEAD_DIM), lambda b, h, i: (h, b * nq + i, 0)),
        out_shape=jax.ShapeDtypeStruct((N_FOX_HEADS, batch * seq, HEAD_DIM), BF16),
        scratch_shapes=[
            pltpu.VMEM((2, FOX_BK, FOX_BQ), F32),
            pltpu.VMEM((1, FOX_BQ), F32),
            pltpu.VMEM((1, FOX_BQ), F32),
            pltpu.VMEM((HEAD_DIM, FOX_BQ), F32),
        ],
        compiler_params=_params(("parallel", "parallel", "parallel")),
        name="fox",
    )(u, u, kx, vt, um, vmt, crow.reshape(batch * N_FOX_HEADS, 1, p))


def _ret_body(decays, rq_ref, rk_ref, rv_ref, rg_ref, mk_ref, mv_ref, cos_ref, sin_ref,
              cosm_ref, sinm_ref, dm_ref, zt_ref, xi_ref, gn_ref, o_ref, s_ref):
    n = pl.program_id(1)
    lane = lax.broadcasted_iota(I32, (BLOCK, HEAD_DIM), 1)
    even = (lane & 1) == 0

    def rope(x, cos, sin):
        rot = jnp.where(even, -pltpu.roll(x, HEAD_DIM - 1, 1), pltpu.roll(x, 1, 1))
        return x * cos + rot * sin

    @pl.when(n == 0)
    def _():
        rowm = lax.broadcasted_iota(I32, (BLOCK, HEAD_DIM), 0)
        for hh in range(N_RET_HEADS):
            km = rope(mk_ref[hh].astype(F32) * SCALE, cosm_ref[...], sinm_ref[...])
            km = jnp.where(rowm >= PAD, km, 0.0)
            kz = (km * zt_ref[hh]).T.astype(BF16)
            s_ref[hh] = jnp.dot(kz, mv_ref[hh], preferred_element_type=F32)

    cos = cos_ref[...]
    sin = sin_ref[...]
    for hh in range(N_RET_HEADS):
        qr = rope(rq_ref[hh].astype(F32), cos, sin)
        kr = rope(rk_ref[hh].astype(F32) * SCALE, cos, sin)
        v = rv_ref[hh]
        sc = lax.dot_general(qr.astype(BF16), kr.astype(BF16), (((1,), (1,)), ((), ())),
                             preferred_element_type=F32) * dm_ref[hh]
        inner = jnp.dot(sc.astype(BF16), v, preferred_element_type=F32)
        s_prev = s_ref[hh]
        cross = jnp.dot((qr * xi_ref[hh]).astype(BF16), s_prev.astype(BF16),
                        preferred_element_type=F32)
        kz = (kr * zt_ref[hh]).T.astype(BF16)
        s_ref[hh] = s_prev * decays[hh] + jnp.dot(kz, v, preferred_element_type=F32)

        o = inner + cross
        mu = jnp.mean(o, axis=-1, keepdims=True)
        d = o - mu
        var = jnp.mean(d * d, axis=-1, keepdims=True)
        on = d * lax.rsqrt(var + EPS) * gn_ref[:, hh * HEAD_DIM:(hh + 1) * HEAD_DIM]
        g = rg_ref[hh].astype(F32)
        silu = g / (1.0 + jnp.exp(-g))
        o_ref[:, hh * HEAD_DIM:(hh + 1) * HEAD_DIM] = (on * silu).astype(BF16)


def _retention(u, um, cos, sin, cosm, sinm, dmask, zeta, xi, decays, gain, batch, seq):
    nc = seq // BLOCK
    hs = (N_RET_HEADS, BLOCK, HEAD_DIM)
    full3 = lambda b, n: (0, 0, 0)
    return pl.pallas_call(
        functools.partial(_ret_body, decays),
        grid=(batch, nc),
        in_specs=[
            pl.BlockSpec(hs, lambda b, n: (SLAB_RQ, b * nc + n, 0)),
            pl.BlockSpec(hs, lambda b, n: (SLAB_RK, b * nc + n, 0)),
            pl.BlockSpec(hs, lambda b, n: (SLAB_RV, b * nc + n, 0)),
            pl.BlockSpec(hs, lambda b, n: (SLAB_RG, b * nc + n, 0)),
            pl.BlockSpec(hs, lambda b, n: (SLAB_RK, 0, 0)),
            pl.BlockSpec(hs, lambda b, n: (SLAB_RV, 0, 0)),
            pl.BlockSpec((BLOCK, HEAD_DIM), lambda b, n: (n, 0)),
            pl.BlockSpec((BLOCK, HEAD_DIM), lambda b, n: (n, 0)),
            pl.BlockSpec((BLOCK, HEAD_DIM), lambda b, n: (0, 0)),
            pl.BlockSpec((BLOCK, HEAD_DIM), lambda b, n: (0, 0)),
            pl.BlockSpec(hs, full3),
            pl.BlockSpec(hs, full3),
            pl.BlockSpec(hs, full3),
            pl.BlockSpec((1, RET_WIDTH), lambda b, n: (0, 0)),
        ],
        out_specs=pl.BlockSpec((BLOCK, RET_WIDTH), lambda b, n: (b * nc + n, 0)),
        out_shape=jax.ShapeDtypeStruct((batch * seq, RET_WIDTH), BF16),
        scratch_shapes=[pltpu.VMEM(hs, F32)],
        compiler_params=_params(("parallel", "arbitrary")),
        name="retention",
    )(u, u, u, u, um, um, cos, sin, cosm, sinm, dmask, zeta, xi, gain)


def _route_body(of_ref, or_ref, x_ref, fg_ref, wo_ref, ng_ref, wr_ref, br_ref, ust_ref,
                h1_ref, hn_ref, ri_ref, rw_ref, cnt_ref, run_ref):
    tm = ROUTE_TM

    @pl.when(pl.program_id(0) == 0)
    def _():
        run_ref[...] = jnp.zeros_like(run_ref)

    of = jnp.concatenate([of_ref[hh].astype(F32) for hh in range(N_FOX_HEADS)], axis=-1)
    ofn = _rms(of, fg_ref[...]).astype(BF16)
    attn = jnp.dot(ofn, wo_ref[0:FOX_WIDTH, :], preferred_element_type=F32)
    attn = attn + jnp.dot(or_ref[...], wo_ref[FOX_WIDTH:, :], preferred_element_type=F32)
    h1 = x_ref[...] + attn
    h1_ref[...] = h1
    hn = _rms(h1, ng_ref[...])
    hn_ref[...] = hn

    hi = hn.astype(BF16)
    lo = (hn - hi.astype(F32)).astype(BF16)
    la = jnp.dot(hi, wr_ref[...], preferred_element_type=F32)
    lb = jnp.dot(lo, wr_ref[...], preferred_element_type=F32)
    logits = la[:, :LANES] + la[:, LANES:] + lb[:, :LANES] + br_ref[...]
    lt = logits.T
    gl = lt[0:N_GROUPS, :]
    el = lt[8:8 + N_EXPERTS, :]

    grow = lax.broadcasted_iota(I32, (N_GROUPS, tm), 0).astype(F32)
    gmax = jnp.max(gl, axis=0, keepdims=True)
    g_star = jnp.min(jnp.where(gl == gmax, grow, float(N_GROUPS)), axis=0, keepdims=True)
    p_group = 1.0 / jnp.sum(jnp.exp(gl - gmax), axis=0, keepdims=True)

    erow_i = lax.broadcasted_iota(I32, (N_EXPERTS, tm), 0)
    erow = erow_i.astype(F32)
    egrp = (erow_i >> 2).astype(F32)
    neg = float("-inf")
    elm = jnp.where(egrp == g_star, el, neg)
    v0 = jnp.max(elm, axis=0, keepdims=True)
    i0 = jnp.min(jnp.where(elm == v0, erow, float(N_EXPERTS)), axis=0, keepdims=True)
    elm2 = jnp.where(erow == i0, neg, elm)
    v1 = jnp.max(elm2, axis=0, keepdims=True)
    i1 = jnp.min(jnp.where(elm2 == v1, erow, float(N_EXPERTS)), axis=0, keepdims=True)
    e10 = jnp.exp(v1 - v0)
    w0 = p_group / (1.0 + e10)
    w1 = p_group * e10 / (1.0 + e10)

    oh0 = erow == i0
    oh1 = erow == i1
    both = jnp.logical_or(oh0, oh1)
    prefix = jnp.dot(both.astype(BF16), ust_ref[...], preferred_element_type=F32)
    base = run_ref[...] + prefix
    rank0 = jnp.sum(jnp.where(oh0, base, 0.0), axis=0, keepdims=True)
    rank1 = jnp.sum(jnp.where(oh1, base, 0.0), axis=0, keepdims=True)
    run_new = run_ref[...] + jnp.sum(both.astype(F32), axis=1, keepdims=True)
    run_ref[...] = run_new
    cnt_ref[...] = run_new

    r8 = lax.broadcasted_iota(I32, (8, tm), 0)
    ri = jnp.where(r8 == 0, i0, jnp.where(r8 == 1, i1, jnp.where(r8 == 2, rank0,
                                                                  jnp.where(r8 == 3, rank1, 0.0))))
    ri_ref[...] = ri.astype(I32)
    r128 = lax.broadcasted_iota(I32, (LANES, tm), 0)
    rw = jnp.where(r128 == 0, w0, jnp.where(r128 == 1, w1, 0.0))
    rw_ref[...] = rw.T


def _route(o_fox, o_ret, x2d, fox_g, w_out, ffn_g, w_router, b_router):
    t = x2d.shape[0]
    tm = ROUTE_TM
    ust = (lax.broadcasted_iota(I32, (tm, tm), 0) < lax.broadcasted_iota(I32, (tm, tm), 1)).astype(BF16)
    const2 = lambda i: (0, 0)
    return pl.pallas_call(
        _route_body,
        grid=(t // tm,),
        in_specs=[
            pl.BlockSpec((N_FOX_HEADS, tm, HEAD_DIM), lambda i: (0, i, 0)),
            pl.BlockSpec((tm, RET_WIDTH), lambda i: (i, 0)),
            pl.BlockSpec((tm, D_MODEL), lambda i: (i, 0)),
            pl.BlockSpec((1, FOX_WIDTH), const2),
            pl.BlockSpec((FOX_WIDTH + RET_WIDTH, D_MODEL), const2),
            pl.BlockSpec((1, D_MODEL), const2),
            pl.BlockSpec((D_MODEL, 2 * LANES), const2),
            pl.BlockSpec((1, LANES), const2),
            pl.BlockSpec((tm, tm), const2),
        ],
        out_specs=[
            pl.BlockSpec((tm, D_MODEL), lambda i: (i, 0)),
            pl.BlockSpec((tm, D_MODEL), lambda i: (i, 0)),
            pl.BlockSpec((8, tm), lambda i: (0, i)),
            pl.BlockSpec((tm, LANES), lambda i: (i, 0)),
            pl.BlockSpec((N_EXPERTS, tm), const2),
        ],
        out_shape=[
            jax.ShapeDtypeStruct((t, D_MODEL), F32),
            jax.ShapeDtypeStruct((t, D_MODEL), F32),
            jax.ShapeDtypeStruct((8, t), I32),
            jax.ShapeDtypeStruct((t, LANES), F32),
            jax.ShapeDtypeStruct((N_EXPERTS, tm), F32),
        ],
        scratch_shapes=[pltpu.VMEM((N_EXPERTS, tm), F32)],
        compiler_params=_params(("arbitrary",)),
        name="route",
    )(o_fox, o_ret, x2d, fox_g, w_out, ffn_g, w_router, b_router, ust)


def _row_copy(src, src_row, dst, dst_row, sem):
    return pltpu.make_async_copy(src.at[pl.ds(src_row, 1)], dst.at[pl.ds(dst_row, 1)], sem)


def _dispatch_body(pos_ref, seg_ref, hn_ref, xs_hbm, zero_ref, sem, zsem):
    @pl.when(pl.program_id(0) == 0)
    def _():
        zero_ref[...] = jnp.zeros_like(zero_ref)

        def tail_copy(t):
            return pltpu.make_async_copy(zero_ref, xs_hbm.at[pl.ds(t * EXPERT_TM, EXPERT_TM)], zsem)

        for wait in (False, True):
            for e in range(N_EXPERTS):
                first = seg_ref[0, e]
                n = seg_ref[1, e]
                off = first + n
                for bit in (128, 64, 32, 16, 8):
                    off = off - (n & bit)

                    @pl.when((n & bit) != 0)
                    def _(off=off, bit=bit):
                        cp = pltpu.make_async_copy(zero_ref.at[pl.ds(0, bit)],
                                                   xs_hbm.at[pl.ds(pl.multiple_of(off, 8), bit)], zsem)
                        cp.wait() if wait else cp.start()

                for r in range(7):
                    @pl.when(r < (n & 7))
                    def _(r=r, first=first):
                        cp = _row_copy(zero_ref, 0, xs_hbm, first + r, zsem)
                        cp.wait() if wait else cp.start()

            def tail(t, carry):
                tail_copy(t).wait() if wait else tail_copy(t).start()
                return carry

            lax.fori_loop(seg_ref[0, N_EXPERTS], seg_ref[1, N_EXPERTS], tail, 0)

    def issue(r, carry):
        _row_copy(hn_ref, r, xs_hbm, pos_ref[0, r], sem).start()
        _row_copy(hn_ref, r, xs_hbm, pos_ref[1, r], sem).start()
        return carry

    lax.fori_loop(0, ROW_TK, issue, 0, unroll=8)
    for _ in range(2):
        pltpu.make_async_copy(hn_ref, xs_hbm.at[pl.ds(0, ROW_TK)], sem).wait()


def _dispatch(pos3, seg, hn, n_rows):
    nt = pos3.shape[0]
    return pl.pallas_call(
        _dispatch_body,
        grid=(nt,),
        in_specs=[
            pl.BlockSpec((None, 2, ROW_TK), lambda i: (i, 0, 0), memory_space=pltpu.SMEM),
            pl.BlockSpec(memory_space=pltpu.SMEM),
            pl.BlockSpec((ROW_TK, D_MODEL), lambda i: (i, 0)),
        ],
        out_specs=pl.BlockSpec(memory_space=pl.ANY),
        out_shape=jax.ShapeDtypeStruct((n_rows, D_MODEL), F32),
        scratch_shapes=[
            pltpu.VMEM((EXPERT_TM, D_MODEL), F32),
            pltpu.SemaphoreType.DMA(()),
            pltpu.SemaphoreType.DMA(()),
        ],
        compiler_params=_params(("arbitrary",)),
        name="dispatch",
    )(pos3, seg, hn)


def _expert_body(te_ref, nu_ref, xs_ref, wg_ref, wu_ref, wd_ref, ys_ref):
    del te_ref
    i = pl.program_id(0)

    @pl.when(i < nu_ref[0])
    def _():
        x = xs_ref[...].astype(BF16)
        g = jnp.dot(x, wg_ref[...], preferred_element_type=F32)
        u = jnp.dot(x, wu_ref[...], preferred_element_type=F32)
        act = (g / (1.0 + jnp.exp(-g)) * u).astype(BF16)
        ys_ref[...] = jnp.dot(act, wd_ref[...], preferred_element_type=F32)

    @pl.when(i >= nu_ref[0])
    def _():
        ys_ref[...] = jnp.zeros(ys_ref.shape, ys_ref.dtype)


def _experts(tile_expert, n_used, xs, wg, wu, wd):
    nt = xs.shape[0] // EXPERT_TM
    grid_spec = pltpu.PrefetchScalarGridSpec(
        num_scalar_prefetch=2,
        grid=(nt,),
        in_specs=[
            pl.BlockSpec((EXPERT_TM, D_MODEL), lambda i, te, nu: (jnp.minimum(i, nu[0] - 1), 0)),
            pl.BlockSpec((None, D_MODEL, D_EXPERT), lambda i, te, nu: (te[i], 0, 0)),
            pl.BlockSpec((None, D_MODEL, D_EXPERT), lambda i, te, nu: (te[i], 0, 0)),
            pl.BlockSpec((None, D_EXPERT, D_MODEL), lambda i, te, nu: (te[i], 0, 0)),
        ],
        out_specs=pl.BlockSpec((EXPERT_TM, D_MODEL), lambda i, te, nu: (i, 0)),
    )
    return pl.pallas_call(
        _expert_body,
        grid_spec=grid_spec,
        out_shape=jax.ShapeDtypeStruct(xs.shape, F32),
        compiler_params=_params(("arbitrary",)),
        name="experts",
    )(tile_expert, n_used, xs, wg, wu, wd)


def _combine_body(pos_ref, posn_ref, h1_ref, rw_ref, g_ref, ys_hbm, o_ref, y_ref, sem):
    i = pl.program_id(0)
    slot = i & 1

    def gather(p_ref, sl):
        def issue(r, carry):
            _row_copy(ys_hbm, p_ref[0, r], y_ref.at[sl, 0], r, sem.at[sl]).start()
            _row_copy(ys_hbm, p_ref[1, r], y_ref.at[sl, 1], r, sem.at[sl]).start()
            return carry

        lax.fori_loop(0, ROW_TK, issue, 0, unroll=8)

    @pl.when(i == 0)
    def _():
        gather(pos_ref, 0)

    @pl.when(i + 1 < pl.num_programs(0))
    def _():
        gather(posn_ref, 1 - slot)

    for s in range(2):
        pltpu.make_async_copy(ys_hbm.at[pl.ds(0, ROW_TK)], y_ref.at[slot, s], sem.at[slot]).wait()

    rw = rw_ref[...]
    moe = rw[:, 0:1] * y_ref[slot, 0] + rw[:, 1:2] * y_ref[slot, 1]
    o_ref[...] = _rms(h1_ref[...] + moe, g_ref[...])


def _combine(pos3, h1, rw, g, ys):
    nt = pos3.shape[0]
    t = h1.shape[0]
    pos_spec = lambda f: pl.BlockSpec((None, 2, ROW_TK), f, memory_space=pltpu.SMEM)
    return pl.pallas_call(
        _combine_body,
        grid=(nt,),
        in_specs=[
            pos_spec(lambda i: (i, 0, 0)),
            pos_spec(lambda i: (jnp.minimum(i + 1, nt - 1), 0, 0)),
            pl.BlockSpec((ROW_TK, D_MODEL), lambda i: (i, 0)),
            pl.BlockSpec((ROW_TK, LANES), lambda i: (i, 0)),
            pl.BlockSpec((1, D_MODEL), lambda i: (0, 0)),
            pl.BlockSpec(memory_space=pl.ANY),
        ],
        out_specs=pl.BlockSpec((ROW_TK, D_MODEL), lambda i: (i, 0)),
        out_shape=jax.ShapeDtypeStruct((t, D_MODEL), F32),
        scratch_shapes=[
            pltpu.VMEM((2, 2, ROW_TK, D_MODEL), F32),
            pltpu.SemaphoreType.DMA((2,)),
        ],
        compiler_params=_params(("arbitrary",)),
        name="combine",
    )(pos3, pos3, h1, rw, g, ys)


def _layer(x2d, meta_blk, batch, seq, attn_norm_g, w_in, b_forget, fox_out_g, ret_out_g, w_out,
           ffn_norm_g, w_rg, b_rg, w_re, b_re, w_gate, w_up, w_down, consts):
    t = batch * seq
    fw = FOX_WIDTH
    c0 = 3 * fw
    c1 = c0 + N_FOX_HEADS
    w_main = jnp.concatenate([w_in[:, :c0], w_in[:, c1:]], axis=1).astype(BF16)
    w_flog = jnp.pad(w_in[:, c0:c1], ((0, 0), (0, LANES - N_FOX_HEADS))).astype(BF16)
    g_attn = attn_norm_g.reshape(1, D_MODEL)

    u, flog, vt = _inproj(x2d, g_attn, w_main, w_flog, INPROJ_TM)
    um, flog_m, vmt = _inproj(meta_blk, g_attn, w_main, w_flog, BLOCK)

    fl_all = jnp.concatenate(
        [jnp.broadcast_to(flog_m[None], (batch, BLOCK, LANES)), flog.reshape(batch, seq, LANES)], axis=1)
    bf_pad = jnp.pad(b_forget.reshape(1, N_FOX_HEADS), ((0, 0), (0, LANES - N_FOX_HEADS)))
    crow, kx = _gates(fl_all, bf_pad)

    o_fox = _fox(u, um, vt, vmt, kx, crow, batch, seq)
    o_ret = _retention(u, um, consts["cos"], consts["sin"], consts["cosm"], consts["sinm"],
                       consts["dmask"], consts["zeta"], consts["xi"], consts["decays"],
                       ret_out_g.reshape(1, RET_WIDTH), batch, seq)

    w_router = jnp.zeros((D_MODEL, LANES), F32)
    w_router = w_router.at[:, 0:N_GROUPS].set(w_rg).at[:, 8:8 + N_EXPERTS].set(w_re)
    b_router = jnp.zeros((1, LANES), F32)
    b_router = b_router.at[0, 0:N_GROUPS].set(b_rg).at[0, 8:8 + N_EXPERTS].set(b_re)
    wr_hi = w_router.astype(BF16)
    wr_lo = (w_router - wr_hi.astype(F32)).astype(BF16)
    h1, hn2, ri, rw, cnt = _route(o_fox, o_ret, x2d, fox_out_g.reshape(1, FOX_WIDTH),
                                  w_out.astype(BF16), ffn_norm_g.reshape(1, D_MODEL),
                                  jnp.concatenate([wr_hi, wr_lo], axis=1), b_router)

    counts = cnt[:, 0].astype(I32)
    padded = ((counts + EXPERT_TM - 1) // EXPERT_TM) * EXPERT_TM
    ends = jnp.cumsum(padded)
    offs = ends - padded
    n_tiles = 2 * t // EXPERT_TM + N_EXPERTS
    n_used = (ends[-1] // EXPERT_TM).astype(I32)
    starts = jnp.arange(n_tiles, dtype=I32) * EXPERT_TM
    te = jnp.sum((starts[:, None] >= ends[None, :]).astype(I32), axis=1)
    last_e = jnp.max(jnp.where(padded > 0, jnp.arange(N_EXPERTS, dtype=I32), 0))
    te = jnp.minimum(te, last_e).astype(I32)
    pos = jnp.stack([offs[ri[0]] + ri[2], offs[ri[1]] + ri[3]], axis=0)
    pos3 = pos.reshape(2, t // ROW_TK, ROW_TK).transpose(1, 0, 2)

    seg = jnp.stack([jnp.append(offs + counts, n_used), jnp.append(padded - counts, n_tiles)]).astype(I32)
    xs = _dispatch(pos3, seg, hn2, n_tiles * EXPERT_TM)
    ys = _experts(te, n_used.reshape(1), xs, w_gate.astype(BF16), w_up.astype(BF16),
                  w_down.astype(BF16))
    return pos3, h1, rw, ys


def _constants(seq):
    lg = jnp.log(1.0 - 2.0 ** (-5.0 - jnp.arange(N_RET_HEADS, dtype=F32)))
    angle = 1.0 / (RET_ROPE_BASE ** jnp.linspace(0.0, 1.0, HEAD_DIM // 2, dtype=F32))
    angle = jnp.repeat(angle, 2)
    pos = (jnp.arange(seq + BLOCK) - PAD).astype(F32)
    phase = pos[:, None] * angle[None, :]
    sin = jnp.sin(phase)
    cos = jnp.cos(phase)
    idx = jnp.arange(BLOCK, dtype=F32)
    diff = idx[:, None] - idx[None, :]
    dmask = jnp.where(diff[None] >= 0, jnp.exp(lg[:, None, None] * jnp.maximum(diff, 0.0)[None]), 0.0)
    zeta = jnp.exp(lg[:, None] * (BLOCK - 1 - idx)[None, :])
    xi = jnp.exp(lg[:, None] * (idx + 1.0)[None, :])
    bshape = (N_RET_HEADS, BLOCK, HEAD_DIM)
    return {
        "cos": cos[BLOCK:], "sin": sin[BLOCK:], "cosm": cos[:BLOCK], "sinm": sin[:BLOCK],
        "dmask": dmask,
        "zeta": jnp.broadcast_to(zeta[:, :, None], bshape),
        "xi": jnp.broadcast_to(xi[:, :, None], bshape),
        "decays": tuple(float((1.0 - 2.0 ** (-5.0 - h)) ** BLOCK) for h in range(N_RET_HEADS)),
    }


def kernel(x, meta_tokens, attn_norm_g, w_in, b_forget, fox_out_g, ret_out_g, w_out, ffn_norm_g,
           w_router_group, b_router_group, w_router_expert, b_router_expert, w_gate, w_up, w_down,
           final_norm_g):
    batch, seq, _ = x.shape
    depth = w_in.shape[0]
    assert depth == 1, "single-layer trunk"
    x2d = x.reshape(batch * seq, D_MODEL)
    meta_blk = jnp.concatenate([jnp.zeros((PAD, D_MODEL), x.dtype), meta_tokens.astype(x.dtype)], axis=0)
    consts = _constants(seq)
    pos3, h1, rw, ys = _layer(
        x2d, meta_blk, batch, seq, attn_norm_g[0], w_in[0], b_forget[0], fox_out_g[0], ret_out_g[0],
        w_out[0], ffn_norm_g[0], w_router_group[0], b_router_group[0], w_router_expert[0],
        b_router_expert[0], w_gate[0], w_up[0], w_down[0], consts)
    out = _combine(pos3, h1, rw, final_norm_g.reshape(1, D_MODEL), ys)
    return out.reshape(batch, seq, D_MODEL)
```

```python
import functools
import math

import jax
import jax.numpy as jnp
from jax import lax
from jax.experimental import pallas as pl
from jax.experimental.pallas import tpu as pltpu

D_MODEL = 2048
N_META = 16
BLOCK = 128
PAD = BLOCK - N_META
HEAD_DIM = 128
N_FOX_HEADS = 8
N_RET_HEADS = 8
FOX_WIDTH = N_FOX_HEADS * HEAD_DIM
RET_WIDTH = N_RET_HEADS * HEAD_DIM
RET_ROPE_BASE = 10000.0
N_GROUPS = 4
EXPERTS_PER_GROUP = 4
N_EXPERTS = N_GROUPS * EXPERTS_PER_GROUP
D_EXPERT = 1024
EPS = 1e-6
MASK_VALUE = -1e30
SCALE = HEAD_DIM ** -0.5
LOG2E = math.log2(math.e)

F32 = jnp.float32
BF16 = jnp.bfloat16
I32 = jnp.int32

LANES = 128
VMEM_LIMIT = 52 * 1024 * 1024

SLAB_FQ, SLAB_FK, SLAB_FV, SLAB_RQ, SLAB_RK, SLAB_RV, SLAB_RG = range(7)
N_SLABS = 7
HEADS_PER_SLAB = 8

INPROJ_TM = 1024
FOX_BQ = 512
FOX_BK = 256
ROUTE_TM = 512
ROW_TK = 256
EXPERT_TM = 256
RET_CHUNKS = 2
GATES_CHUNKS = 11


def _params(sem):
    return pltpu.CompilerParams(dimension_semantics=sem, vmem_limit_bytes=VMEM_LIMIT)


def _rms(x, g):
    ms = jnp.mean(x * x, axis=-1, keepdims=True)
    return x * lax.rsqrt(ms + EPS) * g


def _inproj_body(x_ref, g_ref, w_ref, wf_ref, o_ref, fl_ref, vt_ref, hn_ref):
    j = pl.program_id(1)

    @pl.when(j == 0)
    def _():
        hn = _rms(x_ref[...], g_ref[...]).astype(BF16)
        hn_ref[...] = hn
        fl_ref[...] = jnp.dot(hn, wf_ref[...], preferred_element_type=F32)

    acc = jnp.dot(hn_ref[...], w_ref[...], preferred_element_type=F32)
    for hh in range(HEADS_PER_SLAB):
        o_ref[hh] = acc[:, hh * HEAD_DIM:(hh + 1) * HEAD_DIM].astype(BF16)

    @pl.when(j == SLAB_FV)
    def _():
        for hh in range(HEADS_PER_SLAB):
            vt_ref[hh] = acc[:, hh * HEAD_DIM:(hh + 1) * HEAD_DIM].T.astype(BF16)


def _inproj(x2d, g, w_main, w_flog, tm):
    t = x2d.shape[0]
    tn = HEADS_PER_SLAB * HEAD_DIM
    return pl.pallas_call(
        _inproj_body,
        grid=(t // tm, N_SLABS),
        in_specs=[
            pl.BlockSpec((tm, D_MODEL), lambda i, j: (i, 0)),
            pl.BlockSpec((1, D_MODEL), lambda i, j: (0, 0)),
            pl.BlockSpec((D_MODEL, tn), lambda i, j: (0, j)),
            pl.BlockSpec((D_MODEL, LANES), lambda i, j: (0, 0)),
        ],
        out_specs=[
            pl.BlockSpec((HEADS_PER_SLAB, tm, HEAD_DIM), lambda i, j: (j, i, 0)),
            pl.BlockSpec((tm, LANES), lambda i, j: (i, 0)),
            pl.BlockSpec((HEADS_PER_SLAB, HEAD_DIM, tm), lambda i, j: (0, 0, i)),
        ],
        out_shape=[
            jax.ShapeDtypeStruct((N_SLABS * HEADS_PER_SLAB, t, HEAD_DIM), BF16),
            jax.ShapeDtypeStruct((t, LANES), F32),
            jax.ShapeDtypeStruct((HEADS_PER_SLAB, HEAD_DIM, t), BF16),
        ],
        scratch_shapes=[pltpu.VMEM((tm, D_MODEL), BF16)],
        compiler_params=_params(("parallel", "arbitrary")),
        name="inproj",
    )(x2d, g, w_main, w_flog)


def _split3(x):
    hi = x.astype(BF16)
    r1 = x - hi.astype(F32)
    mid = r1.astype(BF16)
    lo = (r1 - mid.astype(F32)).astype(BF16)
    return hi, mid, lo


def _gates_body(fl_ref, bf_ref, tri_ref, sel_ref, crow_ref, kx_ref, carry_ref):
    c = pl.program_id(1)

    @pl.when(c == 0)
    def _():
        carry_ref[...] = jnp.zeros_like(carry_ref)

    carry = carry_ref[...]
    lane = lax.broadcasted_iota(I32, (BLOCK, LANES), 1)
    for g in range(GATES_CHUNKS):
        rows = slice(g * BLOCK, (g + 1) * BLOCK)
        z = fl_ref[rows, :] + bf_ref[...]
        lf = jnp.minimum(z, 0.0) - jnp.log(1.0 + jnp.exp(-jnp.abs(z)))
        if g == 0:
            row = lax.broadcasted_iota(I32, lf.shape, 0)
            lf = jnp.where(jnp.logical_or(c > 0, row >= PAD), lf, 0.0)
        lf = jnp.where(lane < N_FOX_HEADS, lf, 0.0)
        cs = jnp.dot(tri_ref[...], jnp.concatenate(_split3(lf), axis=1), preferred_element_type=F32)
        cs = cs[:, :LANES] + cs[:, LANES:2 * LANES] + cs[:, 2 * LANES:] + carry
        carry = cs[BLOCK - 1:BLOCK, :]
        cs2 = cs * LOG2E
        crow_ref[:, rows] = cs2.T[0:N_FOX_HEADS, :]
        hi, mid, lo = _split3(cs2)
        packed = (hi.astype(F32) + pltpu.roll(mid.astype(F32), N_FOX_HEADS, 1)
                  + pltpu.roll(lo.astype(F32), 2 * N_FOX_HEADS, 1)).astype(BF16)
        kx = jnp.dot(packed, sel_ref[...], preferred_element_type=F32)
        for hh in range(N_FOX_HEADS):
            kx_ref[hh, rows, :] = kx[:, hh * LANES:(hh + 1) * LANES].astype(BF16)
    carry_ref[...] = carry


def _gates(fl_all, bf_pad):
    b, p, _ = fl_all.shape
    rows = GATES_CHUNKS * BLOCK
    tri = (lax.broadcasted_iota(I32, (BLOCK, BLOCK), 0)
           >= lax.broadcasted_iota(I32, (BLOCK, BLOCK), 1)).astype(BF16)
    r = lax.broadcasted_iota(I32, (LANES, N_FOX_HEADS * LANES), 0)
    cc = lax.broadcasted_iota(I32, (LANES, N_FOX_HEADS * LANES), 1)
    sel = jnp.logical_and(r < 3 * N_FOX_HEADS,
                          cc == (r % N_FOX_HEADS) * LANES + r // N_FOX_HEADS).astype(BF16)
    return pl.pallas_call(
        _gates_body,
        grid=(b, p // rows),
        in_specs=[
            pl.BlockSpec((None, rows, LANES), lambda bi, c: (bi, c, 0)),
            pl.BlockSpec((1, LANES), lambda bi, c: (0, 0)),
            pl.BlockSpec((BLOCK, BLOCK), lambda bi, c: (0, 0)),
            pl.BlockSpec((LANES, N_FOX_HEADS * LANES), lambda bi, c: (0, 0)),
        ],
        out_specs=[
            pl.BlockSpec((None, N_FOX_HEADS, rows), lambda bi, c: (bi, 0, c)),
            pl.BlockSpec((N_FOX_HEADS, rows, LANES), lambda bi, c: (bi, c, 0)),
        ],
        out_shape=[
            jax.ShapeDtypeStruct((b, N_FOX_HEADS, p), F32),
            jax.ShapeDtypeStruct((b * N_FOX_HEADS, p, LANES), BF16),
        ],
        scratch_shapes=[pltpu.VMEM((1, LANES), F32)],
        compiler_params=_params(("parallel", "arbitrary")),
        name="gates",
    )(fl_all, bf_pad, tri, sel)


def _fox_body(q_ref, k_ref, kx_ref, vt_ref, km_ref, vmt_ref, crow_ref, wg_ref, wu_ref, wd_ref,
              o_ref, wgo_ref, wuo_ref, wdo_ref, s_ref, m_ref, l_ref, acc_ref):
    i = pl.program_id(2)
    wgo_ref[...] = wg_ref[...].astype(BF16)
    wuo_ref[...] = wu_ref[...].astype(BF16)
    wdo_ref[...] = wd_ref[...].astype(BF16)
    bq, bk = FOX_BQ, FOX_BK

    qs = (q_ref[...].astype(F32) * (SCALE * LOG2E)).astype(BF16)
    lane = lax.broadcasted_iota(I32, (bq, LANES), 1)
    qx = jnp.where(lane < 3, -1.0, 0.0).astype(BF16)
    qa = jnp.concatenate([qs, qx], axis=1)
    ct = crow_ref[:, pl.ds(pl.multiple_of(BLOCK + i * bq, BLOCK), bq)]

    def qk(kb, kxb):
        ka = jnp.concatenate([kb, kxb], axis=1)
        return lax.dot_general(ka, qa, (((1,), (1,)), ((), ())), preferred_element_type=F32)

    tmeta = qk(km_ref[...], kx_ref[0:BLOCK, :])
    tmeta = jnp.where(lax.broadcasted_iota(I32, (BLOCK, bq), 0) >= PAD, tmeta, MASK_VALUE)
    m0 = jnp.max(tmeta, axis=0, keepdims=True) + ct
    pm = jnp.exp2(tmeta + (ct - m0))
    m_ref[...] = m0
    l_ref[...] = jnp.sum(pm, axis=0, keepdims=True)
    acc_ref[...] = jnp.dot(vmt_ref[...], pm.astype(BF16), preferred_element_type=F32)

    def score_next(u, slot):
        off = pl.multiple_of(u * bk, bk)
        s_ref[slot] = qk(k_ref[pl.ds(off, bk), :],
                         kx_ref[pl.ds(pl.multiple_of(BLOCK + off, BLOCK), bk), :])

    def process(u, slot, shift):
        off = pl.multiple_of(u * bk, bk)
        t = s_ref[slot]
        if shift is not None:
            keep = (lax.broadcasted_iota(I32, (bk, bq), 0) + shift
                    <= lax.broadcasted_iota(I32, (bk, bq), 1))
            t = jnp.where(keep, t, MASK_VALUE)
        m_prev = m_ref[...]
        m_new = jnp.maximum(m_prev, jnp.max(t, axis=0, keepdims=True) + ct)
        alpha = jnp.exp2(m_prev - m_new)
        p = jnp.exp2(t + (ct - m_new))
        l_ref[...] = alpha * l_ref[...] + jnp.sum(p, axis=0, keepdims=True)
        acc_ref[...] = alpha * acc_ref[...] + jnp.dot(vt_ref[:, pl.ds(off, bk)], p.astype(BF16),
                                                      preferred_element_type=F32)
        m_ref[...] = m_new

    score_next(0, 0)

    def pair(jj, carry):
        u0 = 2 * jj
        score_next(u0 + 1, 1)
        process(u0, 0, None)
        score_next(u0 + 2, 0)
        process(u0 + 1, 1, None)
        return carry

    def quad(jj, carry):
        u0 = 4 * jj
        score_next(u0 + 1, 1)
        process(u0, 0, None)
        score_next(u0 + 2, 0)
        process(u0 + 1, 1, None)
        score_next(u0 + 3, 1)
        process(u0 + 2, 0, None)
        score_next(u0 + 4, 0)
        process(u0 + 3, 1, None)
        return carry

    lax.fori_loop(0, i >> 1, quad, 0)

    @pl.when((i & 1) == 1)
    def _():
        pair(i - 1, 0)

    score_next(2 * i + 1, 1)
    process(2 * i, 0, 0)
    process(2 * i + 1, 1, bk)
    o_ref[...] = (acc_ref[...] / l_ref[...]).T.astype(BF16)


def _fox(u, um, vt, vmt, kx, crow, w_gate, w_up, w_down, batch, seq):
    nq = seq // FOX_BQ
    p = seq + BLOCK
    steps = batch * N_FOX_HEADS * nq
    wg2 = w_gate.reshape(-1, D_EXPERT)
    wu2 = w_up.reshape(-1, D_EXPERT)
    wd2 = w_down.reshape(-1, D_MODEL)
    rg, rd = wg2.shape[0] // steps, wd2.shape[0] // steps
    assert rg * steps == wg2.shape[0] and rd * steps == wd2.shape[0] and rd % 16 == 0
    step = lambda b, h, i: ((b * N_FOX_HEADS + h) * nq + i, 0)
    o_fox, wgb, wub, wdb = pl.pallas_call(
        _fox_body,
        grid=(batch, N_FOX_HEADS, nq),
        in_specs=[
            pl.BlockSpec((None, FOX_BQ, HEAD_DIM), lambda b, h, i: (SLAB_FQ * 8 + h, b * nq + i, 0)),
            pl.BlockSpec((None, seq, HEAD_DIM), lambda b, h, i: (SLAB_FK * 8 + h, b, 0)),
            pl.BlockSpec((None, p, LANES), lambda b, h, i: (b * N_FOX_HEADS + h, 0, 0)),
            pl.BlockSpec((None, HEAD_DIM, seq), lambda b, h, i: (h, 0, b)),
            pl.BlockSpec((None, BLOCK, HEAD_DIM), lambda b, h, i: (SLAB_FK * 8 + h, 0, 0)),
            pl.BlockSpec((None, HEAD_DIM, BLOCK), lambda b, h, i: (h, 0, 0)),
            pl.BlockSpec((None, 1, p), lambda b, h, i: (b * N_FOX_HEADS + h, 0, 0)),
            pl.BlockSpec((rg, D_EXPERT), step),
            pl.BlockSpec((rg, D_EXPERT), step),
            pl.BlockSpec((rd, D_MODEL), step),
        ],
        out_specs=[
            pl.BlockSpec((None, FOX_BQ, HEAD_DIM), lambda b, h, i: (h, b * nq + i, 0)),
            pl.BlockSpec((rg, D_EXPERT), step),
            pl.BlockSpec((rg, D_EXPERT), step),
            pl.BlockSpec((rd, D_MODEL), step),
        ],
        out_shape=[
            jax.ShapeDtypeStruct((N_FOX_HEADS, batch * seq, HEAD_DIM), BF16),
            jax.ShapeDtypeStruct(wg2.shape, BF16),
            jax.ShapeDtypeStruct(wu2.shape, BF16),
            jax.ShapeDtypeStruct(wd2.shape, BF16),
        ],
        scratch_shapes=[
            pltpu.VMEM((2, FOX_BK, FOX_BQ), F32),
            pltpu.VMEM((1, FOX_BQ), F32),
            pltpu.VMEM((1, FOX_BQ), F32),
            pltpu.VMEM((HEAD_DIM, FOX_BQ), F32),
        ],
        compiler_params=_params(("parallel", "parallel", "parallel")),
        name="fox",
    )(u, u, kx, vt, um, vmt, crow.reshape(batch * N_FOX_HEADS, 1, p), wg2, wu2, wd2)
    return o_fox, wgb.reshape(w_gate.shape), wub.reshape(w_up.shape), wdb.reshape(w_down.shape)


def _ret_body(decays, rq_ref, rk_ref, rv_ref, rg_ref, mk_ref, mv_ref, cos_ref, sin_ref,
              cosm_ref, sinm_ref, rot_ref, dm_ref, zt_ref, xi_ref, gn_ref, o_ref, s_ref):
    n = pl.program_id(1)

    def rope(xb, cos, sin):
        swapped = jnp.dot(xb, rot_ref[...], preferred_element_type=F32)
        return xb.astype(F32) * cos + swapped * sin

    @pl.when(n == 0)
    def _():
        rowm = lax.broadcasted_iota(I32, (BLOCK, HEAD_DIM), 0)
        for hh in range(N_RET_HEADS):
            km = rope(mk_ref[hh], cosm_ref[...], sinm_ref[...]) * SCALE
            km = jnp.where(rowm >= PAD, km, 0.0)
            kz = (km * zt_ref[hh]).T.astype(BF16)
            s_ref[hh] = jnp.dot(kz, mv_ref[hh], preferred_element_type=F32)

    heads = range(N_RET_HEADS)
    for c in range(RET_CHUNKS):
        rows = slice(c * BLOCK, (c + 1) * BLOCK)
        cos = jnp.concatenate([cos_ref[rows, :]] * N_RET_HEADS, axis=0)
        sin = jnp.concatenate([sin_ref[rows, :]] * N_RET_HEADS, axis=0)
        qr_all = rope(rq_ref[:, rows, :].reshape(N_RET_HEADS * BLOCK, HEAD_DIM), cos, sin)
        kr_all = rope(rk_ref[:, rows, :].reshape(N_RET_HEADS * BLOCK, HEAD_DIM), cos, sin) * SCALE
        qr = [qr_all[hh * BLOCK:(hh + 1) * BLOCK] for hh in heads]
        kr = [kr_all[hh * BLOCK:(hh + 1) * BLOCK] for hh in heads]
        v = [rv_ref[hh, rows, :] for hh in heads]
        sc = [lax.dot_general(qr[hh].astype(BF16), kr[hh].astype(BF16), (((1,), (1,)), ((), ())),
                              preferred_element_type=F32) * dm_ref[hh] for hh in heads]
        s_prev = [s_ref[hh] for hh in heads]
        cross = [jnp.dot((qr[hh] * xi_ref[hh]).astype(BF16), s_prev[hh].astype(BF16),
                         preferred_element_type=F32) for hh in heads]
        inner = [jnp.dot(sc[hh].astype(BF16), v[hh], preferred_element_type=F32) for hh in heads]
        for hh in heads:
            kz = (kr[hh] * zt_ref[hh]).T.astype(BF16)
            s_ref[hh] = s_prev[hh] * decays[hh] + jnp.dot(kz, v[hh], preferred_element_type=F32)
        for hh in heads:
            o = inner[hh] + cross[hh]
            mu = jnp.mean(o, axis=-1, keepdims=True)
            d = o - mu
            var = jnp.mean(d * d, axis=-1, keepdims=True)
            on = d * lax.rsqrt(var + EPS) * gn_ref[:, hh * HEAD_DIM:(hh + 1) * HEAD_DIM]
            g = rg_ref[hh, rows, :].astype(F32)
            silu = g / (1.0 + jnp.exp(-g))
            o_ref[rows, hh * HEAD_DIM:(hh + 1) * HEAD_DIM] = (on * silu).astype(BF16)


def _retention(u, um, cos, sin, cosm, sinm, dmask, zeta, xi, decays, gain, batch, seq):
    rows = RET_CHUNKS * BLOCK
    nc = seq // rows
    hs = (N_RET_HEADS, rows, HEAD_DIM)
    hm = (N_RET_HEADS, BLOCK, HEAD_DIM)
    full3 = lambda b, n: (0, 0, 0)
    r = lax.broadcasted_iota(I32, (HEAD_DIM, HEAD_DIM), 0)
    c = lax.broadcasted_iota(I32, (HEAD_DIM, HEAD_DIM), 1)
    rot = (jnp.where((c == r + 1) & ((r & 1) == 0), 1.0, 0.0)
           - jnp.where((c == r - 1) & ((r & 1) == 1), 1.0, 0.0)).astype(BF16)
    return pl.pallas_call(
        functools.partial(_ret_body, decays),
        grid=(batch, nc),
        in_specs=[
            pl.BlockSpec(hs, lambda b, n: (SLAB_RQ, b * nc + n, 0)),
            pl.BlockSpec(hs, lambda b, n: (SLAB_RK, b * nc + n, 0)),
            pl.BlockSpec(hs, lambda b, n: (SLAB_RV, b * nc + n, 0)),
            pl.BlockSpec(hs, lambda b, n: (SLAB_RG, b * nc + n, 0)),
            pl.BlockSpec(hm, lambda b, n: (SLAB_RK, 0, 0)),
            pl.BlockSpec(hm, lambda b, n: (SLAB_RV, 0, 0)),
            pl.BlockSpec((rows, HEAD_DIM), lambda b, n: (n, 0)),
            pl.BlockSpec((rows, HEAD_DIM), lambda b, n: (n, 0)),
            pl.BlockSpec((BLOCK, HEAD_DIM), lambda b, n: (0, 0)),
            pl.BlockSpec((BLOCK, HEAD_DIM), lambda b, n: (0, 0)),
            pl.BlockSpec((HEAD_DIM, HEAD_DIM), lambda b, n: (0, 0)),
            pl.BlockSpec(hm, full3),
            pl.BlockSpec(hm, full3),
            pl.BlockSpec(hm, full3),
            pl.BlockSpec((1, RET_WIDTH), lambda b, n: (0, 0)),
        ],
        out_specs=pl.BlockSpec((rows, RET_WIDTH), lambda b, n: (b * nc + n, 0)),
        out_shape=jax.ShapeDtypeStruct((batch * seq, RET_WIDTH), BF16),
        scratch_shapes=[pltpu.VMEM(hm, F32)],
        compiler_params=_params(("parallel", "arbitrary")),
        name="retention",
    )(u, u, u, u, um, um, cos, sin, cosm, sinm, rot, dmask, zeta, xi, gain)


def _route_body(of_ref, or_ref, x_ref, fg_ref, wo_ref, ng_ref, wr_ref, br_ref, ust_ref,
                h1_ref, hn_ref, ri_ref, rw_ref, cnt_ref, run_ref):
    tm = ROUTE_TM

    @pl.when(pl.program_id(0) == 0)
    def _():
        run_ref[...] = jnp.zeros_like(run_ref)

    of = jnp.concatenate([of_ref[hh].astype(F32) for hh in range(N_FOX_HEADS)], axis=-1)
    ofn = _rms(of, fg_ref[...]).astype(BF16)
    attn = jnp.dot(ofn, wo_ref[0:FOX_WIDTH, :], preferred_element_type=F32)
    attn = attn + jnp.dot(or_ref[...], wo_ref[FOX_WIDTH:, :], preferred_element_type=F32)
    h1 = x_ref[...] + attn
    h1_ref[...] = h1
    hn = _rms(h1, ng_ref[...])
    hn_ref[...] = hn

    hi = hn.astype(BF16)
    lo = (hn - hi.astype(F32)).astype(BF16)
    la = jnp.dot(hi, wr_ref[...], preferred_element_type=F32)
    lb = jnp.dot(lo, wr_ref[...], preferred_element_type=F32)
    logits = la[:, :LANES] + la[:, LANES:] + lb[:, :LANES] + br_ref[...]
    lt = logits.T
    gl = lt[0:N_GROUPS, :]
    el = lt[8:8 + N_EXPERTS, :]

    grow = lax.broadcasted_iota(I32, (N_GROUPS, tm), 0).astype(F32)
    gmax = jnp.max(gl, axis=0, keepdims=True)
    g_star = jnp.min(jnp.where(gl == gmax, grow, float(N_GROUPS)), axis=0, keepdims=True)
    p_group = 1.0 / jnp.sum(jnp.exp(gl - gmax), axis=0, keepdims=True)

    erow_i = lax.broadcasted_iota(I32, (N_EXPERTS, tm), 0)
    erow = erow_i.astype(F32)
    egrp = (erow_i >> 2).astype(F32)
    neg = float("-inf")
    elm = jnp.where(egrp == g_star, el, neg)
    v0 = jnp.max(elm, axis=0, keepdims=True)
    i0 = jnp.min(jnp.where(elm == v0, erow, float(N_EXPERTS)), axis=0, keepdims=True)
    elm2 = jnp.where(erow == i0, neg, elm)
    v1 = jnp.max(elm2, axis=0, keepdims=True)
    i1 = jnp.min(jnp.where(elm2 == v1, erow, float(N_EXPERTS)), axis=0, keepdims=True)
    e10 = jnp.exp(v1 - v0)
    w0 = p_group / (1.0 + e10)
    w1 = p_group * e10 / (1.0 + e10)

    oh0 = erow == i0
    oh1 = erow == i1
    both = jnp.logical_or(oh0, oh1)
    prefix = jnp.dot(both.astype(BF16), ust_ref[...], preferred_element_type=F32)
    base = run_ref[...] + prefix
    rank0 = jnp.sum(jnp.where(oh0, base, 0.0), axis=0, keepdims=True)
    rank1 = jnp.sum(jnp.where(oh1, base, 0.0), axis=0, keepdims=True)
    run_new = run_ref[...] + jnp.sum(both.astype(F32), axis=1, keepdims=True)
    run_ref[...] = run_new
    cnt_ref[...] = run_new

    r8 = lax.broadcasted_iota(I32, (8, tm), 0)
    ri = jnp.where(r8 == 0, i0, jnp.where(r8 == 1, i1, jnp.where(r8 == 2, rank0,
                                                                  jnp.where(r8 == 3, rank1, 0.0))))
    ri_ref[...] = ri.astype(I32)
    r128 = lax.broadcasted_iota(I32, (LANES, tm), 0)
    rw = jnp.where(r128 == 0, w0, jnp.where(r128 == 1, w1, 0.0))
    rw_ref[...] = rw.T


def _route(o_fox, o_ret, x2d, fox_g, w_out, ffn_g, w_router, b_router):
    t = x2d.shape[0]
    tm = ROUTE_TM
    ust = (lax.broadcasted_iota(I32, (tm, tm), 0) < lax.broadcasted_iota(I32, (tm, tm), 1)).astype(BF16)
    const2 = lambda i: (0, 0)
    return pl.pallas_call(
        _route_body,
        grid=(t // tm,),
        in_specs=[
            pl.BlockSpec((N_FOX_HEADS, tm, HEAD_DIM), lambda i: (0, i, 0)),
            pl.BlockSpec((tm, RET_WIDTH), lambda i: (i, 0)),
            pl.BlockSpec((tm, D_MODEL), lambda i: (i, 0)),
            pl.BlockSpec((1, FOX_WIDTH), const2),
            pl.BlockSpec((FOX_WIDTH + RET_WIDTH, D_MODEL), const2),
            pl.BlockSpec((1, D_MODEL), const2),
            pl.BlockSpec((D_MODEL, 2 * LANES), const2),
            pl.BlockSpec((1, LANES), const2),
            pl.BlockSpec((tm, tm), const2),
        ],
        out_specs=[
            pl.BlockSpec((tm, D_MODEL), lambda i: (i, 0)),
            pl.BlockSpec((tm, D_MODEL), lambda i: (i, 0)),
            pl.BlockSpec((8, tm), lambda i: (0, i)),
            pl.BlockSpec((tm, LANES), lambda i: (i, 0)),
            pl.BlockSpec((N_EXPERTS, tm), const2),
        ],
        out_shape=[
            jax.ShapeDtypeStruct((t, D_MODEL), F32),
            jax.ShapeDtypeStruct((t, D_MODEL), F32),
            jax.ShapeDtypeStruct((8, t), I32),
            jax.ShapeDtypeStruct((t, LANES), F32),
            jax.ShapeDtypeStruct((N_EXPERTS, tm), F32),
        ],
        scratch_shapes=[pltpu.VMEM((N_EXPERTS, tm), F32)],
        compiler_params=_params(("arbitrary",)),
        name="route",
    )(o_fox, o_ret, x2d, fox_g, w_out, ffn_g, w_router, b_router, ust)


def _row_copy(src, src_row, dst, dst_row, sem):
    return pltpu.make_async_copy(src.at[pl.ds(src_row, 1)], dst.at[pl.ds(dst_row, 1)], sem)


def _dispatch_body(pos_ref, seg_ref, hn_ref, xs_hbm, zero_ref, sem, zsem):
    @pl.when(pl.program_id(0) == 0)
    def _():
        zero_ref[...] = jnp.zeros_like(zero_ref)

        def tail_copy(t):
            return pltpu.make_async_copy(zero_ref, xs_hbm.at[pl.ds(t * EXPERT_TM, EXPERT_TM)], zsem)

        for wait in (False, True):
            for e in range(N_EXPERTS):
                first = seg_ref[0, e]
                n = seg_ref[1, e]
                off = first + n
                for bit in (128, 64, 32, 16, 8):
                    off = off - (n & bit)

                    @pl.when((n & bit) != 0)
                    def _(off=off, bit=bit):
                        cp = pltpu.make_async_copy(zero_ref.at[pl.ds(0, bit)],
                                                   xs_hbm.at[pl.ds(pl.multiple_of(off, 8), bit)], zsem)
                        cp.wait() if wait else cp.start()

                for r in range(7):
                    @pl.when(r < (n & 7))
                    def _(r=r, first=first):
                        cp = _row_copy(zero_ref, 0, xs_hbm, first + r, zsem)
                        cp.wait() if wait else cp.start()

            def tail(t, carry):
                tail_copy(t).wait() if wait else tail_copy(t).start()
                return carry

            lax.fori_loop(seg_ref[0, N_EXPERTS], seg_ref[1, N_EXPERTS], tail, 0)

    def issue(r, carry):
        _row_copy(hn_ref, r, xs_hbm, pos_ref[0, r], sem).start()
        _row_copy(hn_ref, r, xs_hbm, pos_ref[1, r], sem).start()
        return carry

    lax.fori_loop(0, ROW_TK, issue, 0, unroll=8)
    for _ in range(2):
        pltpu.make_async_copy(hn_ref, xs_hbm.at[pl.ds(0, ROW_TK)], sem).wait()


def _dispatch(pos3, seg, hn, n_rows):
    nt = pos3.shape[0]
    return pl.pallas_call(
        _dispatch_body,
        grid=(nt,),
        in_specs=[
            pl.BlockSpec((None, 2, ROW_TK), lambda i: (i, 0, 0), memory_space=pltpu.SMEM),
            pl.BlockSpec(memory_space=pltpu.SMEM),
            pl.BlockSpec((ROW_TK, D_MODEL), lambda i: (i, 0)),
        ],
        out_specs=pl.BlockSpec(memory_space=pl.ANY),
        out_shape=jax.ShapeDtypeStruct((n_rows, D_MODEL), F32),
        scratch_shapes=[
            pltpu.VMEM((EXPERT_TM, D_MODEL), F32),
            pltpu.SemaphoreType.DMA(()),
            pltpu.SemaphoreType.DMA(()),
        ],
        compiler_params=_params(("arbitrary",)),
        name="dispatch",
    )(pos3, seg, hn)


def _expert_body(te_ref, nu_ref, xs_ref, wg_ref, wu_ref, wd_ref, ys_ref):
    del te_ref
    i = pl.program_id(0)

    @pl.when(i < nu_ref[0])
    def _():
        x = xs_ref[...].astype(BF16)
        g = jnp.dot(x, wg_ref[...], preferred_element_type=F32)
        u = jnp.dot(x, wu_ref[...], preferred_element_type=F32)
        act = (g / (1.0 + jnp.exp(-g)) * u).astype(BF16)
        ys_ref[...] = jnp.dot(act, wd_ref[...], preferred_element_type=F32)

    @pl.when(i >= nu_ref[0])
    def _():
        ys_ref[...] = jnp.zeros(ys_ref.shape, ys_ref.dtype)


def _experts(tile_expert, n_used, xs, wg, wu, wd):
    nt = xs.shape[0] // EXPERT_TM
    grid_spec = pltpu.PrefetchScalarGridSpec(
        num_scalar_prefetch=2,
        grid=(nt,),
        in_specs=[
            pl.BlockSpec((EXPERT_TM, D_MODEL), lambda i, te, nu: (jnp.minimum(i, nu[0] - 1), 0)),
            pl.BlockSpec((None, D_MODEL, D_EXPERT), lambda i, te, nu: (te[i], 0, 0)),
            pl.BlockSpec((None, D_MODEL, D_EXPERT), lambda i, te, nu: (te[i], 0, 0)),
            pl.BlockSpec((None, D_EXPERT, D_MODEL), lambda i, te, nu: (te[i], 0, 0)),
        ],
        out_specs=pl.BlockSpec((EXPERT_TM, D_MODEL), lambda i, te, nu: (i, 0)),
    )
    return pl.pallas_call(
        _expert_body,
        grid_spec=grid_spec,
        out_shape=jax.ShapeDtypeStruct(xs.shape, F32),
        compiler_params=_params(("arbitrary",)),
        name="experts",
    )(tile_expert, n_used, xs, wg, wu, wd)


def _combine_body(pos_ref, posn_ref, h1_ref, rw_ref, g_ref, ys_hbm, o_ref, y_ref, sem):
    i = pl.program_id(0)
    slot = i & 1

    def gather(p_ref, sl):
        def issue(r, carry):
            _row_copy(ys_hbm, p_ref[0, r], y_ref.at[sl, 0], r, sem.at[sl]).start()
            _row_copy(ys_hbm, p_ref[1, r], y_ref.at[sl, 1], r, sem.at[sl]).start()
            return carry

        lax.fori_loop(0, ROW_TK, issue, 0, unroll=8)

    @pl.when(i == 0)
    def _():
        gather(pos_ref, 0)

    @pl.when(i + 1 < pl.num_programs(0))
    def _():
        gather(posn_ref, 1 - slot)

    for s in range(2):
        pltpu.make_async_copy(ys_hbm.at[pl.ds(0, ROW_TK)], y_ref.at[slot, s], sem.at[slot]).wait()

    rw = rw_ref[...]
    moe = rw[:, 0:1] * y_ref[slot, 0] + rw[:, 1:2] * y_ref[slot, 1]
    o_ref[...] = _rms(h1_ref[...] + moe, g_ref[...])


def _combine(pos3, h1, rw, g, ys):
    nt = pos3.shape[0]
    t = h1.shape[0]
    pos_spec = lambda f: pl.BlockSpec((None, 2, ROW_TK), f, memory_space=pltpu.SMEM)
    return pl.pallas_call(
        _combine_body,
        grid=(nt,),
        in_specs=[
            pos_spec(lambda i: (i, 0, 0)),
            pos_spec(lambda i: (jnp.minimum(i + 1, nt - 1), 0, 0)),
            pl.BlockSpec((ROW_TK, D_MODEL), lambda i: (i, 0)),
            pl.BlockSpec((ROW_TK, LANES), lambda i: (i, 0)),
            pl.BlockSpec((1, D_MODEL), lambda i: (0, 0)),
            pl.BlockSpec(memory_space=pl.ANY),
        ],
        out_specs=pl.BlockSpec((ROW_TK, D_MODEL), lambda i: (i, 0)),
        out_shape=jax.ShapeDtypeStruct((t, D_MODEL), F32),
        scratch_shapes=[
            pltpu.VMEM((2, 2, ROW_TK, D_MODEL), F32),
            pltpu.SemaphoreType.DMA((2,)),
        ],
        compiler_params=_params(("arbitrary",)),
        name="combine",
    )(pos3, pos3, h1, rw, g, ys)


def _layer(x2d, meta_blk, batch, seq, attn_norm_g, w_in, b_forget, fox_out_g, ret_out_g, w_out,
           ffn_norm_g, w_rg, b_rg, w_re, b_re, w_gate, w_up, w_down, consts):
    t = batch * seq
    fw = FOX_WIDTH
    c0 = 3 * fw
    c1 = c0 + N_FOX_HEADS
    w_main = jnp.concatenate([w_in[:, :c0], w_in[:, c1:]], axis=1).astype(BF16)
    w_flog = jnp.pad(w_in[:, c0:c1], ((0, 0), (0, LANES - N_FOX_HEADS))).astype(BF16)
    g_attn = attn_norm_g.reshape(1, D_MODEL)

    u, flog, vt = _inproj(x2d, g_attn, w_main, w_flog, INPROJ_TM)
    um, flog_m, vmt = _inproj(meta_blk, g_attn, w_main, w_flog, BLOCK)

    fl_all = jnp.concatenate(
        [jnp.broadcast_to(flog_m[None], (batch, BLOCK, LANES)), flog.reshape(batch, seq, LANES)], axis=1)
    bf_pad = jnp.pad(b_forget.reshape(1, N_FOX_HEADS), ((0, 0), (0, LANES - N_FOX_HEADS)))
    crow, kx = _gates(fl_all, bf_pad)

    o_fox, wg_bf, wu_bf, wd_bf = _fox(u, um, vt, vmt, kx, crow, w_gate, w_up, w_down, batch, seq)
    o_ret = _retention(u, um, consts["cos"], consts["sin"], consts["cosm"], consts["sinm"],
                       consts["dmask"], consts["zeta"], consts["xi"], consts["decays"],
                       ret_out_g.reshape(1, RET_WIDTH), batch, seq)

    w_router = jnp.zeros((D_MODEL, LANES), F32)
    w_router = w_router.at[:, 0:N_GROUPS].set(w_rg).at[:, 8:8 + N_EXPERTS].set(w_re)
    b_router = jnp.zeros((1, LANES), F32)
    b_router = b_router.at[0, 0:N_GROUPS].set(b_rg).at[0, 8:8 + N_EXPERTS].set(b_re)
    wr_hi = w_router.astype(BF16)
    wr_lo = (w_router - wr_hi.astype(F32)).astype(BF16)
    h1, hn2, ri, rw, cnt = _route(o_fox, o_ret, x2d, fox_out_g.reshape(1, FOX_WIDTH),
                                  w_out.astype(BF16), ffn_norm_g.reshape(1, D_MODEL),
                                  jnp.concatenate([wr_hi, wr_lo], axis=1), b_router)

    counts = cnt[:, 0].astype(I32)
    padded = ((counts + EXPERT_TM - 1) // EXPERT_TM) * EXPERT_TM
    ends = jnp.cumsum(padded)
    offs = ends - padded
    n_tiles = 2 * t // EXPERT_TM + N_EXPERTS
    n_used = (ends[-1] // EXPERT_TM).astype(I32)
    starts = jnp.arange(n_tiles, dtype=I32) * EXPERT_TM
    te = jnp.sum((starts[:, None] >= ends[None, :]).astype(I32), axis=1)
    last_e = jnp.max(jnp.where(padded > 0, jnp.arange(N_EXPERTS, dtype=I32), 0))
    te = jnp.minimum(te, last_e).astype(I32)
    pos = jnp.stack([offs[ri[0]] + ri[2], offs[ri[1]] + ri[3]], axis=0)
    pos3 = pos.reshape(2, t // ROW_TK, ROW_TK).transpose(1, 0, 2)

    seg = jnp.stack([jnp.append(offs + counts, n_used), jnp.append(padded - counts, n_tiles)]).astype(I32)
    xs = _dispatch(pos3, seg, hn2, n_tiles * EXPERT_TM)
    ys = _experts(te, n_used.reshape(1), xs, wg_bf, wu_bf, wd_bf)
    return pos3, h1, rw, ys


def _constants(seq):
    lg = jnp.log(1.0 - 2.0 ** (-5.0 - jnp.arange(N_RET_HEADS, dtype=F32)))
    angle = 1.0 / (RET_ROPE_BASE ** jnp.linspace(0.0, 1.0, HEAD_DIM // 2, dtype=F32))
    angle = jnp.repeat(angle, 2)
    pos = (jnp.arange(seq + BLOCK) - PAD).astype(F32)
    phase = pos[:, None] * angle[None, :]
    sin = jnp.sin(phase)
    cos = jnp.cos(phase)
    idx = jnp.arange(BLOCK, dtype=F32)
    diff = idx[:, None] - idx[None, :]
    dmask = jnp.where(diff[None] >= 0, jnp.exp(lg[:, None, None] * jnp.maximum(diff, 0.0)[None]), 0.0)
    zeta = jnp.exp(lg[:, None] * (BLOCK - 1 - idx)[None, :])
    xi = jnp.exp(lg[:, None] * (idx + 1.0)[None, :])
    bshape = (N_RET_HEADS, BLOCK, HEAD_DIM)
    return {
        "cos": cos[BLOCK:], "sin": sin[BLOCK:], "cosm": cos[:BLOCK], "sinm": sin[:BLOCK],
        "dmask": dmask,
        "zeta": jnp.broadcast_to(zeta[:, :, None], bshape),
        "xi": jnp.broadcast_to(xi[:, :, None], bshape),
        "decays": tuple(float((1.0 - 2.0 ** (-5.0 - h)) ** BLOCK) for h in range(N_RET_HEADS)),
    }


def kernel(x, meta_tokens, attn_norm_g, w_in, b_forget, fox_out_g, ret_out_g, w_out, ffn_norm_g,
           w_router_group, b_router_group, w_router_expert, b_router_expert, w_gate, w_up, w_down,
           final_norm_g):
    batch, seq, _ = x.shape
    depth = w_in.shape[0]
    assert depth == 1, "single-layer trunk"
    x2d = x.reshape(batch * seq, D_MODEL)
    meta_blk = jnp.concatenate([jnp.zeros((PAD, D_MODEL), x.dtype), meta_tokens.astype(x.dtype)], axis=0)
    consts = _constants(seq)
    pos3, h1, rw, ys = _layer(
        x2d, meta_blk, batch, seq, attn_norm_g[0], w_in[0], b_forget[0], fox_out_g[0], ret_out_g[0],
        w_out[0], ffn_norm_g[0], w_router_group[0], b_router_group[0], w_router_expert[0],
        b_router_expert[0], w_gate[0], w_up[0], w_down[0], consts)
    out = _combine(pos3, h1, rw, final_norm_g.reshape(1, D_MODEL), ys)
    return out.reshape(batch, seq, D_MODEL)
```
